```python
import math
import jax
import jax.numpy as jnp
from jax import lax
import numpy as np

D_MODEL = 1024
BATCH = 8
SEQ = 2048
DEPTH = 2
DEC_BATCH = 16
DEC_SEQ = 4096
PAST_LEN = 128

HEAD_DIM = 64
A_GROUPS = 4
A_WIDTH = A_GROUPS * HEAD_DIM
CHUNK = 128
B_HEADS = 4
B_WIDTH = B_HEADS * 2 * HEAD_DIM
C_HEADS = 4
C_WIDTH = C_HEADS * HEAD_DIM
MIX_WIDTH = A_WIDTH + B_WIDTH + C_WIDTH
IN_WIDTH = 2 * A_WIDTH + 3 * B_WIDTH + 3 * C_WIDTH
GRID_W = 64
NA_ROWS_MAX = 8
NA_COLS = 16
RPB_ROWS = 2 * NA_ROWS_MAX - 1
RPB_COLS = 2 * NA_COLS - 1
D_FF = ((8 * D_MODEL // 3 + 255) // 256) * 256
DEEPNORM_ALPHA = (2 * DEPTH) ** 0.25
DEEPNORM_BETA = (8 * DEPTH) ** -0.25
LN_EPS = 1e-5
Q_BLOCK = 128

kernel_name = 'hybrid_gmlp_diffattn_natten_encoder'


def layer_norm(x, g, b):
    xf = x.astype(jnp.float32)
    mu = jnp.mean(xf, axis=-1, keepdims=True)
    xc = xf - mu
    var = jnp.mean(jnp.square(xc), axis=-1, keepdims=True)
    return (xc * lax.rsqrt(var + LN_EPS) * g + b).astype(x.dtype)


def rms_norm(x, g):
    xf = x.astype(jnp.float32)
    ms = jnp.mean(jnp.square(xf), axis=-1, keepdims=True)
    return (xf * lax.rsqrt(ms + LN_EPS) * g).astype(x.dtype)


def split_projection(h):
    sizes = [A_WIDTH, A_WIDTH, B_WIDTH, B_WIDTH, B_WIDTH, C_WIDTH, C_WIDTH, C_WIDTH]
    idx = np.cumsum(sizes)[:-1].tolist()
    return jnp.split(h, idx, axis=-1)


def spatial_gating(u, v, ln_g, ln_b, w_s, b_s):
    bsz, s, _ = u.shape
    v = layer_norm(v, ln_g, ln_b)
    vc = v.reshape(bsz, s // CHUNK, CHUNK, A_GROUPS, HEAD_DIM)
    mixed = jnp.einsum('gts,bnsgc->bntgc', w_s, vc) + jnp.transpose(b_s)[None, None, :, :, None]
    return u * mixed.reshape(bsz, s, A_WIDTH)


def differential_attention(q, k, v, lam, lam_init, subln_g):
    bsz, s = q.shape[0], q.shape[1]
    nb = s // Q_BLOCK
    scale = HEAD_DIM ** -0.5
    slopes = jnp.exp2(-8.0 * jnp.arange(1, B_HEADS + 1, dtype=jnp.float32) / B_HEADS)
    kpos = jnp.arange(s)
    qb = jnp.transpose((q * scale).reshape(bsz, nb, Q_BLOCK, B_HEADS, 2, HEAD_DIM), (1, 0, 2, 3, 4, 5))

    def block(args):
        qi, i = args
        sc = jnp.einsum('bqhcd,bkhcd->bchqk', qi, k, preferred_element_type=jnp.float32)
        qpos = i * Q_BLOCK + jnp.arange(Q_BLOCK)
        dist = jnp.abs(qpos[:, None] - kpos[None, :]).astype(jnp.float32)
        p = jax.nn.softmax(sc - slopes[:, None, None] * dist, axis=-1)
        attn = p[:, 0] - lam * p[:, 1]
        return jnp.einsum('bhqk,bkhe->bqhe', attn.astype(v.dtype), v)

    out = lax.map(block, (qb, jnp.arange(nb)))
    out = jnp.transpose(out, (1, 0, 2, 3, 4)).reshape(bsz, s, B_HEADS, 2 * HEAD_DIM)
    out = rms_norm(out, subln_g) * (1.0 - lam_init)
    return out.reshape(bsz, s, B_WIDTH)


def neighbourhood_attention(q, k, v, rpb):
    bsz, s = q.shape[0], q.shape[1]
    rows = s // GRID_W
    wr = min(NA_ROWS_MAX, rows)
    wc = NA_COLS
    scale = HEAD_DIM ** -0.5
    qg = jnp.transpose((q * scale).reshape(bsz, rows, GRID_W, C_HEADS, HEAD_DIM), (1, 0, 2, 3, 4))
    kg = k.reshape(bsz, rows, GRID_W, C_HEADS, HEAD_DIM)
    vg = v.reshape(bsz, rows, GRID_W, C_HEADS, HEAD_DIM)
    col = jnp.arange(GRID_W)
    col_start = jnp.clip(col - wc // 2, 0, GRID_W - wc)
    key_cols = col_start[:, None] + jnp.arange(wc)[None, :]
    col_off = key_cols - col[:, None] + (NA_COLS - 1)

    def row_block(args):
        qr, r = args
        r_start = jnp.clip(r - wr // 2, 0, rows - wr)
        kr = lax.dynamic_slice_in_dim(kg, r_start, wr, axis=1)
        vr = lax.dynamic_slice_in_dim(vg, r_start, wr, axis=1)
        kn = kr[:, :, key_cols]
        vn = vr[:, :, key_cols]
        row_off = r_start + jnp.arange(wr) - r + (NA_ROWS_MAX - 1)
        bias = rpb[:, row_off[:, None, None], col_off[None, :, :]]
        sc = jnp.einsum('bqhd,biqjhd->bhqij', qr, kn, preferred_element_type=jnp.float32)
        sc = sc + jnp.transpose(bias, (0, 2, 1, 3)).astype(jnp.float32)[None]
        p = jax.nn.softmax(sc.reshape(bsz, C_HEADS, GRID_W, wr * wc), axis=-1)
        p = p.reshape(bsz, C_HEADS, GRID_W, wr, wc)
        return jnp.einsum('bhqij,biqjhd->bqhd', p.astype(vn.dtype), vn)

    out = lax.map(row_block, (qg, jnp.arange(rows)))
    return jnp.transpose(out, (1, 0, 2, 3, 4)).reshape(bsz, s, C_WIDTH)


def token_mix(x, layer, w_in, w_out, a_ln_g, a_ln_b, a_w_s, a_b_s, lq1, lk1, lq2, lk2, subln_g, rpb):
    bsz, s, _ = x.shape
    h = x @ w_in
    a_u, a_v, b_q, b_k, b_v, c_q, c_k, c_v = split_projection(h)
    out_a = spatial_gating(jax.nn.gelu(a_u), jax.nn.gelu(a_v), a_ln_g, a_ln_b, a_w_s, a_b_s)
    lam_init = 0.8 - 0.6 * math.exp(-0.3 * layer)
    lam = (jnp.exp(jnp.sum(lq1.astype(jnp.float32) * lk1.astype(jnp.float32)))
           - jnp.exp(jnp.sum(lq2.astype(jnp.float32) * lk2.astype(jnp.float32))) + lam_init)
    out_b = differential_attention(
        b_q.reshape(bsz, s, B_HEADS, 2, HEAD_DIM),
        b_k.reshape(bsz, s, B_HEADS, 2, HEAD_DIM),
        b_v.reshape(bsz, s, B_HEADS, 2 * HEAD_DIM),
        lam, lam_init, subln_g)
    out_c = neighbourhood_attention(
        c_q.reshape(bsz, s, C_HEADS, HEAD_DIM),
        c_k.reshape(bsz, s, C_HEADS, HEAD_DIM),
        c_v.reshape(bsz, s, C_HEADS, HEAD_DIM), rpb)
    return jnp.concatenate([out_a, out_b, out_c], axis=-1) @ w_out


def swiglu(x, w_gate, w_up, w_down):
    return (jax.nn.silu(x @ w_gate) * (x @ w_up)) @ w_down


def trunk(x, w_in, w_out, a_ln_g, a_ln_b, a_w_s, a_b_s, b_lambda_q1, b_lambda_k1, b_lambda_q2,
          b_lambda_k2, b_subln_g, c_rpb, w_gate, w_up, w_down, ln_g, ln_b):
    for l in range(DEPTH):
        mix = token_mix(x, l, w_in[l], w_out[l], a_ln_g[l], a_ln_b[l], a_w_s[l], a_b_s[l],
                        b_lambda_q1[l], b_lambda_k1[l], b_lambda_q2[l], b_lambda_k2[l],
                        b_subln_g[l], c_rpb[l])
        x = layer_norm(DEEPNORM_ALPHA * x + mix, ln_g[l, 0], ln_b[l, 0])
        x = layer_norm(DEEPNORM_ALPHA * x + swiglu(x, w_gate[l], w_up[l], w_down[l]), ln_g[l, 1], ln_b[l, 1])
    return x


def setup_inputs(seed: int = 0) -> dict:
    key = jax.random.key(seed)
    ks = jax.random.split(key, 20)
    f32 = jnp.float32

    def nrm(k, shape, scale):
        return jax.random.normal(k, shape, f32) * scale

    x_prompt = nrm(ks[0], (BATCH, SEQ, D_MODEL), 1.0)
    x_sample = nrm(ks[1], (DEC_BATCH, DEC_SEQ, D_MODEL), 1.0)
    col_scale = jnp.concatenate([
        jnp.ones((2 * A_WIDTH + 2 * B_WIDTH,), f32),
        jnp.full((B_WIDTH,), DEEPNORM_BETA, f32),
        jnp.ones((2 * C_WIDTH,), f32),
        jnp.full((C_WIDTH,), DEEPNORM_BETA, f32)])
    w_in = nrm(ks[2], (DEPTH, D_MODEL, IN_WIDTH), D_MODEL ** -0.5) * col_scale
    w_out = nrm(ks[3], (DEPTH, MIX_WIDTH, D_MODEL), MIX_WIDTH ** -0.5 * DEEPNORM_BETA)
    a_ln_g = 1.0 + nrm(ks[4], (DEPTH, A_WIDTH), 0.02)
    a_ln_b = nrm(ks[5], (DEPTH, A_WIDTH), 0.02)
    a_w_s = nrm(ks[6], (DEPTH, A_GROUPS, CHUNK, CHUNK), CHUNK ** -0.5)
    a_b_s = 1.0 + nrm(ks[7], (DEPTH, A_GROUPS, CHUNK), 0.02)
    b_lambda_q1 = nrm(ks[8], (DEPTH, HEAD_DIM), 0.1)
    b_lambda_k1 = nrm(ks[9], (DEPTH, HEAD_DIM), 0.1)
    b_lambda_q2 = nrm(ks[10], (DEPTH, HEAD_DIM), 0.1)
    b_lambda_k2 = nrm(ks[11], (DEPTH, HEAD_DIM), 0.1)
    b_subln_g = 1.0 + nrm(ks[12], (DEPTH, 2 * HEAD_DIM), 0.02)
    c_rpb = nrm(ks[13], (DEPTH, C_HEADS, RPB_ROWS, RPB_COLS), 0.02)
    w_gate = nrm(ks[14], (DEPTH, D_MODEL, D_FF), D_MODEL ** -0.5 * DEEPNORM_BETA)
    w_up = nrm(ks[15], (DEPTH, D_MODEL, D_FF), D_MODEL ** -0.5 * DEEPNORM_BETA)
    w_down = nrm(ks[16], (DEPTH, D_FF, D_MODEL), D_FF ** -0.5 * DEEPNORM_BETA)
    ln_g = 1.0 + nrm(ks[17], (DEPTH, 2, D_MODEL), 0.02)
    ln_b = nrm(ks[18], (DEPTH, 2, D_MODEL), 0.02)
    return {'x_prompt': x_prompt, 'x_sample': x_sample, 'w_in': w_in, 'w_out': w_out,
            'a_ln_g': a_ln_g, 'a_ln_b': a_ln_b, 'a_w_s': a_w_s, 'a_b_s': a_b_s,
            'b_lambda_q1': b_lambda_q1, 'b_lambda_k1': b_lambda_k1,
            'b_lambda_q2': b_lambda_q2, 'b_lambda_k2': b_lambda_k2,
            'b_subln_g': b_subln_g, 'c_rpb': c_rpb,
            'w_gate': w_gate, 'w_up': w_up, 'w_down': w_down,
            'ln_g': ln_g, 'ln_b': ln_b}


def reference(x_prompt, x_sample, w_in, w_out, a_ln_g, a_ln_b, a_w_s, a_b_s, b_lambda_q1, b_lambda_k1,
              b_lambda_q2, b_lambda_k2, b_subln_g, c_rpb, w_gate, w_up, w_down, ln_g, ln_b):
    y_prompt = trunk(x_prompt, w_in, w_out, a_ln_g, a_ln_b, a_w_s, a_b_s, b_lambda_q1, b_lambda_k1,
                     b_lambda_q2, b_lambda_k2, b_subln_g, c_rpb, w_gate, w_up, w_down, ln_g, ln_b)
    y_sample = trunk(x_sample, w_in, w_out, a_ln_g, a_ln_b, a_w_s, a_b_s, b_lambda_q1, b_lambda_k1,
                     b_lambda_q2, b_lambda_k2, b_subln_g, c_rpb, w_gate, w_up, w_down, ln_g, ln_b)
    return (y_prompt, y_sample)
```

```python
import functools
import math

import jax
import jax.numpy as jnp
import numpy as np
from jax import lax
from jax.experimental import pallas as pl
from jax.experimental.pallas import tpu as pltpu

F32 = jnp.float32
BF16 = jnp.bfloat16

D_MODEL = 1024
DEPTH = 2
HEAD_DIM = 64
A_GROUPS = 4
A_WIDTH = A_GROUPS * HEAD_DIM
CHUNK = 128
B_HEADS = 4
B_WIDTH = B_HEADS * 2 * HEAD_DIM
C_HEADS = 4
C_WIDTH = C_HEADS * HEAD_DIM
MIX_WIDTH = A_WIDTH + B_WIDTH + C_WIDTH
IN_WIDTH = 2 * A_WIDTH + 3 * B_WIDTH + 3 * C_WIDTH
GRID_W = 64
NA_ROWS_MAX = 8
NA_COLS = 16
D_FF = 2816
DEEPNORM_ALPHA = (2 * DEPTH) ** 0.25
LN_EPS = 1e-5
QK_SCALE = HEAD_DIM ** -0.5

LANES = 128
VMEM_LIMIT_BYTES = 56 * 1024 * 1024

OFF_AU, OFF_AV = 0, A_WIDTH
OFF_B = 2 * A_WIDTH
OFF_C = OFF_B + 3 * B_WIDTH

NA_Q_ROWS = 4
NA_WIN_ROWS = 12
NEG_BIG = -1e30

FF_CHUNKS = (768, 768, 768, 512)
assert sum(FF_CHUNKS) == D_FF


def _layer_norm(x, g, b):
    mu = jnp.mean(x, axis=-1, keepdims=True)
    xc = x - mu
    var = jnp.mean(xc * xc, axis=-1, keepdims=True)
    return xc * lax.rsqrt(var + LN_EPS) * g + b


def _dot(a, b):
    return jnp.dot(a, b, preferred_element_type=F32)


def _dot_nt(a, b):
    return lax.dot_general(a, b, (((1,), (1,)), ((), ())), preferred_element_type=F32)


def _inproj_kernel(x_ref, w_ref, lng_ref, lnb_ref, ws_ref, bs_ref, oa_ref, hb_ref, hc_ref, *, tm):
    xb = x_ref[...].astype(BF16)
    hb_ref[...] = _dot(xb, w_ref[:, OFF_B:OFF_C]).astype(BF16)
    hc_ref[...] = _dot(xb, w_ref[:, OFF_C:IN_WIDTH]).astype(BF16)

    ha = jax.nn.gelu(_dot(xb, w_ref[:, 0:OFF_B]))
    u = ha[:, :A_WIDTH]
    v = _layer_norm(ha[:, A_WIDTH:], lng_ref[...], lnb_ref[...]).astype(BF16)
    lane = lax.broadcasted_iota(jnp.int32, (CHUNK, A_WIDTH), 1)
    ws = ws_ref[...]
    for n in range(tm // CHUNK):
        rows = slice(n * CHUNK, (n + 1) * CHUNK)
        res = _dot(ws, v[rows, :])
        mixed = res[0:CHUNK]
        for g in range(1, A_GROUPS):
            mixed = jnp.where(lane >= g * HEAD_DIM, res[g * CHUNK:(g + 1) * CHUNK], mixed)
        oa_ref[rows, :] = (u[rows, :] * (mixed + bs_ref[...])).astype(BF16)


def _inproj(x2d, w_in_b, a_ln_g, a_ln_b, ws_all, bs_tab, *, tm):
    n = x2d.shape[0]
    const = lambda i: (0, 0)
    return pl.pallas_call(
        functools.partial(_inproj_kernel, tm=tm),
        grid=(n // tm,),
        in_specs=[
            pl.BlockSpec((tm, D_MODEL), lambda i: (i, 0)),
            pl.BlockSpec((D_MODEL, IN_WIDTH), const),
            pl.BlockSpec((1, A_WIDTH), const),
            pl.BlockSpec((1, A_WIDTH), const),
            pl.BlockSpec((A_GROUPS * CHUNK, CHUNK), const),
            pl.BlockSpec((CHUNK, A_WIDTH), const),
        ],
        out_specs=[
            pl.BlockSpec((tm, A_WIDTH), lambda i: (i, 0)),
            pl.BlockSpec((tm, 3 * B_WIDTH), lambda i: (i, 0)),
            pl.BlockSpec((tm, 3 * C_WIDTH), lambda i: (i, 0)),
        ],
        out_shape=[
            jax.ShapeDtypeStruct((n, A_WIDTH), BF16),
            jax.ShapeDtypeStruct((n, 3 * B_WIDTH), BF16),
            jax.ShapeDtypeStruct((n, 3 * C_WIDTH), BF16),
        ],
        compiler_params=pltpu.CompilerParams(
            dimension_semantics=("parallel",), vmem_limit_bytes=VMEM_LIMIT_BYTES),
        name="inproj_gating",
    )(x2d, w_in_b, a_ln_g, a_ln_b, ws_all, bs_tab)


def _diffattn_kernel(slopes_ref, q_ref, k_ref, v_ref, lq1_ref, lk1_ref, lq2_ref, lk2_ref, g_ref,
                     o_ref, acc0_ref, acc1_ref, *, tq, tk, seq, lam_init):
    h = pl.program_id(1)
    qi = pl.program_id(2)
    slope = slopes_ref[h]

    q = q_ref[...]
    lane = lax.broadcasted_iota(jnp.int32, (tq, LANES), 1)
    zero = jnp.zeros_like(q)
    q0 = jnp.where(lane < HEAD_DIM, q, zero)
    q1 = jnp.where(lane >= HEAD_DIM, q, zero)
    rel = (lax.broadcasted_iota(jnp.int32, (tq, tk), 0)
           - lax.broadcasted_iota(jnp.int32, (tq, tk), 1))

    acc0_ref[...] = jnp.zeros_like(acc0_ref)
    acc1_ref[...] = jnp.zeros_like(acc1_ref)

    def step(s, m, l, acc_ref, vb):
        m_new = jnp.maximum(m, jnp.max(s, axis=-1, keepdims=True))
        p = jnp.exp(s - m_new)
        a = jnp.exp(m - m_new)
        l_new = a * l + jnp.sum(p, axis=-1, keepdims=True)
        acc_ref[...] = a * acc_ref[...] + _dot(p.astype(BF16), vb)
        return m_new, l_new

    def body(j, carry):
        m0, l0, m1, l1 = carry
        start = pl.multiple_of(j * tk, tk)
        kb = k_ref[pl.ds(start, tk), :]
        vb = v_ref[pl.ds(start, tk), :]
        bias = slope * jnp.abs(rel + (qi * tq - j * tk)).astype(F32)
        m0, l0 = step(_dot_nt(q0, kb) - bias, m0, l0, acc0_ref, vb)
        m1, l1 = step(_dot_nt(q1, kb) - bias, m1, l1, acc1_ref, vb)
        return m0, l0, m1, l1

    neg = jnp.full((tq, 1), -jnp.inf, F32)
    zer = jnp.zeros((tq, 1), F32)
    _, l0, _, l1 = lax.fori_loop(0, seq // tk, body, (neg, zer, neg, zer))

    lam = (jnp.exp(jnp.sum(lq1_ref[...] * lk1_ref[...], axis=-1, keepdims=True))
           - jnp.exp(jnp.sum(lq2_ref[...] * lk2_ref[...], axis=-1, keepdims=True)) + lam_init)
    o = acc0_ref[...] / l0 - lam * (acc1_ref[...] / l1)
    ms = jnp.mean(o * o, axis=-1, keepdims=True)
    o_ref[...] = (o * lax.rsqrt(ms + LN_EPS) * g_ref[...] * (1.0 - lam_init)).astype(o_ref.dtype)


def _diffattn(hb, slopes, lq1, lk1, lq2, lk2, subln_g, *, bsz, seq, lam_init, tq, tk):
    n = bsz * seq
    nq = seq // tq
    const = lambda b, h, i, sl: (0, 0)
    grid_spec = pltpu.PrefetchScalarGridSpec(
        num_scalar_prefetch=1,
        grid=(bsz, B_HEADS, nq),
        in_specs=[
            pl.BlockSpec((tq, LANES), lambda b, h, i, sl: (b * nq + i, h)),
            pl.BlockSpec((seq, LANES), lambda b, h, i, sl: (b, B_HEADS + h)),
            pl.BlockSpec((seq, LANES), lambda b, h, i, sl: (b, 2 * B_HEADS + h)),
            pl.BlockSpec((1, HEAD_DIM), const),
            pl.BlockSpec((1, HEAD_DIM), const),
            pl.BlockSpec((1, HEAD_DIM), const),
            pl.BlockSpec((1, HEAD_DIM), const),
            pl.BlockSpec((1, 2 * HEAD_DIM), const),
        ],
        out_specs=pl.BlockSpec((tq, LANES), lambda b, h, i, sl: (b * nq + i, h)),
        scratch_shapes=[pltpu.VMEM((tq, LANES), F32), pltpu.VMEM((tq, LANES), F32)],
    )
    return pl.pallas_call(
        functools.partial(_diffattn_kernel, tq=tq, tk=tk, seq=seq, lam_init=lam_init),
        grid_spec=grid_spec,
        out_shape=jax.ShapeDtypeStruct((n, B_WIDTH), BF16),
        compiler_params=pltpu.CompilerParams(
            dimension_semantics=("parallel", "parallel", "parallel"),
            vmem_limit_bytes=VMEM_LIMIT_BYTES),
        name="diff_attention",
    )(slopes, hb, hb, hb, lq1, lk1, lq2, lk2, subln_g)


def _na_tables(rows):
    wr = min(NA_ROWS_MAX, rows)
    nblk = rows // NA_Q_ROWS
    col = np.arange(GRID_W)
    col_start = np.clip(col - NA_COLS // 2, 0, GRID_W - NA_COLS)
    kc = np.arange(GRID_W)
    col_valid = (kc[None, :] >= col_start[:, None]) & (kc[None, :] < col_start[:, None] + NA_COLS)
    col_off = np.clip(kc[None, :] - col[:, None] + (NA_COLS - 1), 0, 2 * NA_COLS - 2)

    wstart, types, type_of = [], [], []
    for b in range(nblk):
        r = NA_Q_ROWS * b + np.arange(NA_Q_ROWS)
        r_start = np.clip(r - wr // 2, 0, rows - wr)
        w = int(min(r_start[0], rows - NA_WIN_ROWS))
        kr = w + np.arange(NA_WIN_ROWS)
        row_valid = (kr[None, :] >= r_start[:, None]) & (kr[None, :] < r_start[:, None] + wr)
        assert (row_valid.sum(axis=1) == wr).all()
        row_off = np.clip(kr[None, :] - r[:, None] + (NA_ROWS_MAX - 1), 0, 2 * NA_ROWS_MAX - 2)
        key = (row_off.tobytes(), row_valid.tobytes())
        for t, (k0, _, _) in enumerate(types):
            if k0 == key:
                break
        else:
            t = len(types)
            types.append((key, row_off, row_valid))
        wstart.append(w * GRID_W)
        type_of.append(t)
    row_off = np.stack([t[1] for t in types])
    row_valid = np.stack([t[2] for t in types])
    ro = np.broadcast_to(row_off[:, :, None, :, None], row_off.shape[:2] + (GRID_W, NA_WIN_ROWS, GRID_W))
    co = np.broadcast_to(col_off[None, None, :, None, :], ro.shape)
    valid = row_valid[:, :, None, :, None] & col_valid[None, None, :, None, :]
    return (np.asarray(wstart, np.int32), np.asarray(type_of, np.int32),
            np.ascontiguousarray(ro), np.ascontiguousarray(co), valid)


def _na_bias_table(rpb, ro, co, valid):
    t = ro.shape[0]
    gathered = rpb[:, ro, co].astype(F32)
    tab = jnp.where(valid[None], gathered, NEG_BIG)
    tab = jnp.transpose(tab, (1, 0, 2, 3, 4, 5))
    return tab.reshape(t, C_HEADS, NA_Q_ROWS * GRID_W, NA_WIN_ROWS * GRID_W)


def _natten_kernel(wstart_ref, type_ref, q_ref, k_ref, v_ref, tab_ref, o_ref, *, tq, win):
    del type_ref
    i = pl.program_id(1)
    start = pl.multiple_of(wstart_ref[i], GRID_W)
    lane = lax.broadcasted_iota(jnp.int32, (tq, LANES), 1)
    low = lane < HEAD_DIM
    for pair in range(C_HEADS // 2):
        cols = slice(pair * LANES, (pair + 1) * LANES)
        qp = q_ref[:, cols]
        kp = k_ref[pl.ds(start, win), cols]
        vp = v_ref[pl.ds(start, win), cols]
        zero = jnp.zeros_like(qp)
        outs = []
        for hh in range(2):
            qz = jnp.where(low if hh == 0 else jnp.logical_not(low), qp, zero)
            s = _dot_nt(qz, kp) + tab_ref[0, 2 * pair + hh]
            m = jnp.max(s, axis=-1, keepdims=True)
            p = jnp.exp(s - m)
            l = jnp.sum(p, axis=-1, keepdims=True)
            outs.append(_dot(p.astype(BF16), vp) / l)
        o_ref[:, cols] = jnp.where(low, outs[0], outs[1]).astype(o_ref.dtype)


def _natten(hc, tab, wstart, type_of, *, bsz, seq):
    n = bsz * seq
    tq = NA_Q_ROWS * GRID_W
    win = NA_WIN_ROWS * GRID_W
    nblk = seq // tq
    grid_spec = pltpu.PrefetchScalarGridSpec(
        num_scalar_prefetch=2,
        grid=(bsz, nblk),
        in_specs=[
            pl.BlockSpec((tq, C_WIDTH), lambda b, i, ws, ty: (b * nblk + i, 0)),
            pl.BlockSpec((seq, C_WIDTH), lambda b, i, ws, ty: (b, 1)),
            pl.BlockSpec((seq, C_WIDTH), lambda b, i, ws, ty: (b, 2)),
            pl.BlockSpec((1, C_HEADS, tq, win), lambda b, i, ws, ty: (ty[i], 0, 0, 0)),
        ],
        out_specs=pl.BlockSpec((tq, C_WIDTH), lambda b, i, ws, ty: (b * nblk + i, 0)),
    )
    return pl.pallas_call(
        functools.partial(_natten_kernel, tq=tq, win=win),
        grid_spec=grid_spec,
        out_shape=jax.ShapeDtypeStruct((n, C_WIDTH), BF16),
        compiler_params=pltpu.CompilerParams(
            dimension_semantics=("parallel", "parallel"), vmem_limit_bytes=VMEM_LIMIT_BYTES),
        name="neighbourhood_attention",
    )(wstart, type_of, hc, hc, hc, tab)


def _outffn_kernel(x_ref, oa_ref, ob_ref, oc_ref, wo_ref, wg_ref, wu_ref, wd_ref, lng_ref, lnb_ref, y_ref):
    mix = (_dot(oa_ref[...], wo_ref[0:A_WIDTH, :])
           + _dot(ob_ref[...], wo_ref[A_WIDTH:A_WIDTH + B_WIDTH, :])
           + _dot(oc_ref[...], wo_ref[A_WIDTH + B_WIDTH:MIX_WIDTH, :]))
    x1 = _layer_norm(DEEPNORM_ALPHA * x_ref[...] + mix, lng_ref[0:1, :], lnb_ref[0:1, :])
    x1b = x1.astype(BF16)
    ff = None
    off = 0
    for width in FF_CHUNKS:
        cols = slice(off, off + width)
        gate = _dot(x1b, wg_ref[:, cols])
        up = _dot(x1b, wu_ref[:, cols])
        part = _dot((jax.nn.silu(gate) * up).astype(BF16), wd_ref[cols, :])
        ff = part if ff is None else ff + part
        off += width
    y_ref[...] = _layer_norm(DEEPNORM_ALPHA * x1 + ff, lng_ref[1:2, :], lnb_ref[1:2, :])


def _outffn(x2d, oa, ob, oc, wo, wg, wu, wd, ln_g, ln_b, *, tm):
    n = x2d.shape[0]
    const = lambda i: (0, 0)
    row = lambda i: (i, 0)
    resident = functools.partial(pl.BlockSpec, index_map=const, pipeline_mode=pl.Buffered(1))
    return pl.pallas_call(
        _outffn_kernel,
        grid=(n // tm,),
        in_specs=[
            pl.BlockSpec((tm, D_MODEL), row),
            pl.BlockSpec((tm, A_WIDTH), row),
            pl.BlockSpec((tm, B_WIDTH), row),
            pl.BlockSpec((tm, C_WIDTH), row),
            resident((MIX_WIDTH, D_MODEL)),
            resident((D_MODEL, D_FF)),
            resident((D_MODEL, D_FF)),
            resident((D_FF, D_MODEL)),
            pl.BlockSpec((2, D_MODEL), const),
            pl.BlockSpec((2, D_MODEL), const),
        ],
        out_specs=pl.BlockSpec((tm, D_MODEL), row),
        out_shape=jax.ShapeDtypeStruct((n, D_MODEL), F32),
        compiler_params=pltpu.CompilerParams(
            dimension_semantics=("parallel",), vmem_limit_bytes=VMEM_LIMIT_BYTES),
        name="outproj_ffn",
    )(x2d, oa, ob, oc, wo, wg, wu, wd, ln_g, ln_b)


def _prepare_layer(l, w_in, w_out, a_ln_g, a_ln_b, a_w_s, a_b_s, c_rpb, w_gate, w_up, w_down):
    col_scale = np.ones((IN_WIDTH,), np.float32)
    col_scale[OFF_B:OFF_B + B_WIDTH] = QK_SCALE
    col_scale[OFF_C:OFF_C + C_WIDTH] = QK_SCALE
    return dict(
        w_in=(w_in[l] * col_scale).astype(BF16),
        w_out=w_out[l].astype(BF16),
        a_ln_g=a_ln_g[l][None, :],
        a_ln_b=a_ln_b[l][None, :],
        ws_all=a_w_s[l].reshape(A_GROUPS * CHUNK, CHUNK).astype(BF16),
        bs_tab=jnp.repeat(jnp.transpose(a_b_s[l]), HEAD_DIM, axis=1),
        rpb=c_rpb[l],
        w_gate=w_gate[l].astype(BF16),
        w_up=w_up[l].astype(BF16),
        w_down=w_down[l].astype(BF16),
    )


def _trunk(x, layers, lam_params, subln_g, ln_g, ln_b, slopes):
    bsz, seq, _ = x.shape
    rows = seq // GRID_W
    wstart, type_of, ro, co, valid = _na_tables(rows)
    x2d = x.reshape(bsz * seq, D_MODEL)
    for l, p in enumerate(layers):
        lam_init = 0.8 - 0.6 * math.exp(-0.3 * l)
        oa, hb, hc = _inproj(x2d, p["w_in"], p["a_ln_g"], p["a_ln_b"], p["ws_all"], p["bs_tab"], tm=512)
        lq1, lk1, lq2, lk2 = (a[l][None, :] for a in lam_params)
        ob = _diffattn(hb, slopes, lq1, lk1, lq2, lk2, subln_g[l][None, :],
                       bsz=bsz, seq=seq, lam_init=lam_init, tq=256, tk=512)
        tab = _na_bias_table(p["rpb"], ro, co, valid)
        oc = _natten(hc, tab, jnp.asarray(wstart), jnp.asarray(type_of), bsz=bsz, seq=seq)
        x2d = _outffn(x2d, oa, ob, oc, p["w_out"], p["w_gate"], p["w_up"], p["w_down"],
                      ln_g[l], ln_b[l], tm=512)
    return x2d.reshape(bsz, seq, D_MODEL)


def kernel(x_prompt, x_sample, w_in, w_out, a_ln_g, a_ln_b, a_w_s, a_b_s, b_lambda_q1, b_lambda_k1,
           b_lambda_q2, b_lambda_k2, b_subln_g, c_rpb, w_gate, w_up, w_down, ln_g, ln_b):
    layers = [_prepare_layer(l, w_in, w_out, a_ln_g, a_ln_b, a_w_s, a_b_s, c_rpb, w_gate, w_up, w_down)
              for l in range(DEPTH)]
    lam_params = (b_lambda_q1, b_lambda_k1, b_lambda_q2, b_lambda_k2)
    slopes = jnp.exp2(-8.0 * jnp.arange(1, B_HEADS + 1, dtype=F32) / B_HEADS)
    y_prompt = _trunk(x_prompt, layers, lam_params, b_subln_g, ln_g, ln_b, slopes)
    y_sample = _trunk(x_sample, layers, lam_params, b_subln_g, ln_g, ln_b, slopes)
    return (y_prompt, y_sample)
```

```python
import functools
import math

import jax
import jax.numpy as jnp
import numpy as np
from jax import lax
from jax.experimental import pallas as pl
from jax.experimental.pallas import tpu as pltpu

F32 = jnp.float32
BF16 = jnp.bfloat16

D_MODEL = 1024
DEPTH = 2
HEAD_DIM = 64
A_GROUPS = 4
A_WIDTH = A_GROUPS * HEAD_DIM
CHUNK = 128
B_HEADS = 4
B_WIDTH = B_HEADS * 2 * HEAD_DIM
C_HEADS = 4
C_WIDTH = C_HEADS * HEAD_DIM
MIX_WIDTH = A_WIDTH + B_WIDTH + C_WIDTH
IN_WIDTH = 2 * A_WIDTH + 3 * B_WIDTH + 3 * C_WIDTH
GRID_W = 64
NA_ROWS_MAX = 8
NA_COLS = 16
D_FF = 2816
DEEPNORM_ALPHA = (2 * DEPTH) ** 0.25
LN_EPS = 1e-5
QK_SCALE = HEAD_DIM ** -0.5
LOG2E = math.log2(math.e)

LANES = 128
VMEM_LIMIT_BYTES = 56 * 1024 * 1024

OFF_AU, OFF_AV = 0, A_WIDTH
OFF_B = 2 * A_WIDTH
OFF_C = OFF_B + 3 * B_WIDTH

NA_Q_ROWS = 4
NA_WIN_ROWS = 12
NEG_BIG = -1e30

FF_CHUNKS = (768, 768, 768, 512)
assert sum(FF_CHUNKS) == D_FF


def _layer_norm(x, g, b):
    mu = jnp.mean(x, axis=-1, keepdims=True)
    xc = x - mu
    var = jnp.mean(xc * xc, axis=-1, keepdims=True)
    return xc * lax.rsqrt(var + LN_EPS) * g + b


def _dot(a, b):
    return jnp.dot(a, b, preferred_element_type=F32)


def _dot_nt(a, b):
    return lax.dot_general(a, b, (((1,), (1,)), ((), ())), preferred_element_type=F32)


def _inproj_kernel(x_ref, w_ref, lng_ref, lnb_ref, ws_ref, bs_ref, oa_ref, hb_ref, hc_ref, *, tm):
    xb = x_ref[...].astype(BF16)
    hb_ref[...] = _dot(xb, w_ref[:, OFF_B:OFF_C]).astype(BF16)
    hc_ref[...] = _dot(xb, w_ref[:, OFF_C:IN_WIDTH]).astype(BF16)

    ha = jax.nn.gelu(_dot(xb, w_ref[:, 0:OFF_B]))
    u = ha[:, :A_WIDTH]
    v = _layer_norm(ha[:, A_WIDTH:], lng_ref[...], lnb_ref[...]).astype(BF16)
    lane = lax.broadcasted_iota(jnp.int32, (CHUNK, A_WIDTH), 1)
    ws = ws_ref[...]
    for n in range(tm // CHUNK):
        rows = slice(n * CHUNK, (n + 1) * CHUNK)
        res = _dot(ws, v[rows, :])
        mixed = res[0:CHUNK]
        for g in range(1, A_GROUPS):
            mixed = jnp.where(lane >= g * HEAD_DIM, res[g * CHUNK:(g + 1) * CHUNK], mixed)
        oa_ref[rows, :] = (u[rows, :] * (mixed + bs_ref[...])).astype(BF16)


def _inproj(x2d, w_in_b, a_ln_g, a_ln_b, ws_all, bs_tab, *, tm):
    n = x2d.shape[0]
    const = lambda i: (0, 0)
    return pl.pallas_call(
        functools.partial(_inproj_kernel, tm=tm),
        grid=(n // tm,),
        in_specs=[
            pl.BlockSpec((tm, D_MODEL), lambda i: (i, 0)),
            pl.BlockSpec((D_MODEL, IN_WIDTH), const),
            pl.BlockSpec((1, A_WIDTH), const),
            pl.BlockSpec((1, A_WIDTH), const),
            pl.BlockSpec((A_GROUPS * CHUNK, CHUNK), const),
            pl.BlockSpec((CHUNK, A_WIDTH), const),
        ],
        out_specs=[
            pl.BlockSpec((tm, A_WIDTH), lambda i: (i, 0)),
            pl.BlockSpec((tm, 3 * B_WIDTH), lambda i: (i, 0)),
            pl.BlockSpec((tm, 3 * C_WIDTH), lambda i: (i, 0)),
        ],
        out_shape=[
            jax.ShapeDtypeStruct((n, A_WIDTH), BF16),
            jax.ShapeDtypeStruct((n, 3 * B_WIDTH), BF16),
            jax.ShapeDtypeStruct((n, 3 * C_WIDTH), BF16),
        ],
        compiler_params=pltpu.CompilerParams(
            dimension_semantics=("parallel",), vmem_limit_bytes=VMEM_LIMIT_BYTES),
        name="inproj_gating",
    )(x2d, w_in_b, a_ln_g, a_ln_b, ws_all, bs_tab)


def _split3(x):
    hi = x.astype(BF16).astype(F32)
    r = x - hi
    mid = r.astype(BF16).astype(F32)
    return hi, mid, r - mid


def _aug_lanes(x, lane, value_first, one):
    l6 = jnp.bitwise_and(lane, HEAD_DIM - 1)
    hi, mid, lo = _split3(x)
    v0 = 0 if value_first else 3
    out = jnp.where(l6 == v0, hi, jnp.where(l6 == v0 + 1, mid, jnp.where(l6 == v0 + 2, lo, 0.0)))
    ones = (l6 >= 3 - v0) & (l6 < 6 - v0)
    return jnp.where(ones, one, out)


def _diffattn_kernel(slopes_ref, q_ref, k_ref, v_ref, lq1_ref, lk1_ref, lq2_ref, lk2_ref, g_ref,
                     o_ref, ka0_ref, ka1_ref, va_ref, acc0_ref, acc1_ref, sa_ref, sb_ref, mxa_ref, mxb_ref,
                     *, tq, tk, seq, lam_init):
    h = pl.program_id(1)
    qi = pl.program_id(2)
    c = slopes_ref[h] * LOG2E
    fill = 512
    lane_f = lax.broadcasted_iota(jnp.int32, (fill, LANES), 1)
    row_f = lax.broadcasted_iota(jnp.int32, (fill, LANES), 0)

    @pl.when(qi == 0)
    def _():
        def fill_body(r, carry):
            start = pl.multiple_of(r * fill, fill)
            pos = (row_f + r * fill).astype(F32)
            aug = _aug_lanes(c * pos, lane_f, False, 1.0).astype(BF16)
            kk = k_ref[pl.ds(start, fill), :]
            ka0_ref[pl.ds(start, fill), :] = jnp.where(lane_f < HEAD_DIM, kk, aug)
            ka1_ref[pl.ds(start, fill), :] = jnp.where(lane_f < HEAD_DIM, aug, kk)
            va_ref[pl.ds(start, fill), 0:LANES] = v_ref[pl.ds(start, fill), :]
            va_ref[pl.ds(start, fill), LANES:2 * LANES] = jnp.ones((fill, LANES), BF16)
            return carry
        lax.fori_loop(0, seq // fill, fill_body, 0)

    q = q_ref[...]
    lane = lax.broadcasted_iota(jnp.int32, (tq, LANES), 1)
    low = lane < HEAD_DIM
    t = (lax.broadcasted_iota(jnp.int32, (tq, LANES), 0) + qi * tq).astype(F32)
    aug_before = _aug_lanes(-(c * t), lane, True, 1.0)
    zero = jnp.zeros_like(q)

    def variants(aug):
        return jnp.where(low, q, aug), jnp.where(low, aug, q)

    q_before = variants(aug_before.astype(BF16))
    q_after = variants((-aug_before).astype(BF16))
    q_diag = variants(zero)

    acc0_ref[...] = jnp.zeros_like(acc0_ref)
    acc1_ref[...] = jnp.zeros_like(acc1_ref)

    def step(s, mx, m, acc_ref, vb):
        m_new = jnp.maximum(m, jnp.max(mx, axis=-1, keepdims=True))
        p = jnp.exp2(s - m_new)
        a = jnp.exp2(m - m_new)
        acc_ref[...] = a * acc_ref[...] + _dot(p.astype(BF16), vb)
        return m_new

    nkv = seq // tk
    jd = (qi * tq) // tk

    def kv_start(i):
        j = i + (i >= jd).astype(jnp.int32)
        return j, pl.multiple_of(j * tk, tk)

    def scores_into(s_ref, mx_ref, start, qv, bias=None):
        for c2, ka_ref in enumerate((ka0_ref, ka1_ref)):
            s = _dot_nt(qv[c2], ka_ref[pl.ds(start, tk), :])
            if bias is not None:
                s = s - bias
            s_ref[c2] = s
            mx = s[:, 0:LANES]
            for n in range(1, tk // LANES):
                mx = jnp.maximum(mx, s[:, n * LANES:(n + 1) * LANES])
            mx_ref[c2] = mx

    def scores_offdiag(i, s_ref, mx_ref):
        j, start = kv_start(i)
        before = j < jd
        qv = tuple(jnp.where(before, qb, qa) for qb, qa in zip(q_before, q_after))
        scores_into(s_ref, mx_ref, start, qv)

    def consume(s_ref, mx_ref, start, carry):
        m0, m1 = carry
        vb = va_ref[pl.ds(start, tk), :]
        m0 = step(s_ref[0], mx_ref[0], m0, acc0_ref, vb)
        m1 = step(s_ref[1], mx_ref[1], m1, acc1_ref, vb)
        return m0, m1

    def pair(t, carry):
        i0 = 2 * t
        scores_offdiag(i0 + 1, sb_ref, mxb_ref)
        carry = consume(sa_ref, mxa_ref, kv_start(i0)[1], carry)
        scores_offdiag(i0 + 2, sa_ref, mxa_ref)
        return consume(sb_ref, mxb_ref, kv_start(i0 + 1)[1], carry)

    neg = jnp.full((tq, 1), -jnp.inf, F32)
    scores_offdiag(jnp.int32(0), sa_ref, mxa_ref)
    carry = lax.fori_loop(0, nkv // 2 - 1, pair, (neg, neg))
    diag_start = pl.multiple_of(jd * tk, tk)
    rel = (lax.broadcasted_iota(jnp.int32, (tq, tk), 0)
           - lax.broadcasted_iota(jnp.int32, (tq, tk), 1))
    diag_bias = c * jnp.abs(rel + (qi * tq - jd * tk)).astype(F32)
    scores_into(sb_ref, mxb_ref, diag_start, q_diag, diag_bias)
    carry = consume(sa_ref, mxa_ref, kv_start(jnp.int32(nkv - 2))[1], carry)
    consume(sb_ref, mxb_ref, diag_start, carry)

    lam = (jnp.exp(jnp.sum(lq1_ref[...] * lk1_ref[...], axis=-1, keepdims=True))
           - jnp.exp(jnp.sum(lq2_ref[...] * lk2_ref[...], axis=-1, keepdims=True)) + lam_init)
    o = (acc0_ref[:, 0:LANES] / acc0_ref[:, LANES:2 * LANES]
         - lam * (acc1_ref[:, 0:LANES] / acc1_ref[:, LANES:2 * LANES]))
    ms = jnp.mean(o * o, axis=-1, keepdims=True)
    o_ref[...] = (o * lax.rsqrt(ms + LN_EPS) * g_ref[...] * (1.0 - lam_init)).astype(o_ref.dtype)


def _diffattn(hb, slopes, lq1, lk1, lq2, lk2, subln_g, *, bsz, seq, lam_init, tq, tk):
    n = bsz * seq
    nq = seq // tq
    const = lambda b, h, i, sl: (0, 0)
    grid_spec = pltpu.PrefetchScalarGridSpec(
        num_scalar_prefetch=1,
        grid=(bsz, B_HEADS, nq),
        in_specs=[
            pl.BlockSpec((tq, LANES), lambda b, h, i, sl: (b * nq + i, h)),
            pl.BlockSpec((seq, LANES), lambda b, h, i, sl: (b, B_HEADS + h)),
            pl.BlockSpec((seq, LANES), lambda b, h, i, sl: (b, 2 * B_HEADS + h)),
            pl.BlockSpec((1, HEAD_DIM), const),
            pl.BlockSpec((1, HEAD_DIM), const),
            pl.BlockSpec((1, HEAD_DIM), const),
            pl.BlockSpec((1, HEAD_DIM), const),
            pl.BlockSpec((1, 2 * HEAD_DIM), const),
        ],
        out_specs=pl.BlockSpec((tq, LANES), lambda b, h, i, sl: (b * nq + i, h)),
        scratch_shapes=[pltpu.VMEM((seq, LANES), BF16), pltpu.VMEM((seq, LANES), BF16),
                        pltpu.VMEM((seq, 2 * LANES), BF16),
                        pltpu.VMEM((tq, 2 * LANES), F32), pltpu.VMEM((tq, 2 * LANES), F32),
                        pltpu.VMEM((2, tq, tk), F32), pltpu.VMEM((2, tq, tk), F32),
                        pltpu.VMEM((2, tq, LANES), F32), pltpu.VMEM((2, tq, LANES), F32)],
    )
    assert seq % tk == 0 and tk % tq == 0 and (seq // tk) % 2 == 0
    return pl.pallas_call(
        functools.partial(_diffattn_kernel, tq=tq, tk=tk, seq=seq, lam_init=lam_init),
        grid_spec=grid_spec,
        out_shape=jax.ShapeDtypeStruct((n, B_WIDTH), BF16),
        compiler_params=pltpu.CompilerParams(
            dimension_semantics=("arbitrary", "arbitrary", "arbitrary"),
            vmem_limit_bytes=VMEM_LIMIT_BYTES),
        name="diff_attention",
    )(slopes, hb, hb, hb, lq1, lk1, lq2, lk2, subln_g)


def _na_tables(rows):
    wr = min(NA_ROWS_MAX, rows)
    nblk = rows // NA_Q_ROWS
    col = np.arange(GRID_W)
    col_start = np.clip(col - NA_COLS // 2, 0, GRID_W - NA_COLS)
    kc = np.arange(GRID_W)
    col_valid = (kc[None, :] >= col_start[:, None]) & (kc[None, :] < col_start[:, None] + NA_COLS)

    assert rows >= NA_WIN_ROWS and rows % NA_Q_ROWS == 0
    wstart, types, type_of = [], [], []
    for b in range(nblk):
        r = NA_Q_ROWS * b + np.arange(NA_Q_ROWS)
        r_start = np.clip(r - wr // 2, 0, rows - wr)
        w = int(min(r_start[0], rows - NA_WIN_ROWS))
        kr = w + np.arange(NA_WIN_ROWS)
        row_valid = (kr[None, :] >= r_start[:, None]) & (kr[None, :] < r_start[:, None] + wr)
        assert (row_valid.sum(axis=1) == wr).all()
        row_off0 = w - r + (NA_ROWS_MAX - 1)
        key = (row_off0.tobytes(), row_valid.tobytes())
        for t, (k0, _, _) in enumerate(types):
            if k0 == key:
                break
        else:
            t = len(types)
            types.append((key, row_off0, row_valid))
        wstart.append(w * GRID_W)
        type_of.append(t)
    return (np.asarray(wstart, np.int32), np.asarray(type_of, np.int32),
            [(t[1], t[2]) for t in types], col_valid)


def _na_bias_table(rpb, types, col_valid):
    nrow = rpb.shape[1]
    rp = jnp.pad(rpb.astype(F32) * LOG2E, ((0, 0), (0, 0), (GRID_W, GRID_W)))
    per_col = jnp.stack([rp[:, :, GRID_W + NA_COLS - 1 - c: 2 * GRID_W + NA_COLS - 1 - c]
                         for c in range(GRID_W)], axis=2)
    per_col = jnp.where(col_valid[None, None], per_col, NEG_BIG)
    per_col = jnp.pad(per_col, ((0, 0), (NA_WIN_ROWS, NA_WIN_ROWS), (0, 0), (0, 0)), constant_values=NEG_BIG)
    tabs = []
    for row_off0, row_valid in types:
        blocks = []
        for j in range(NA_Q_ROWS):
            lo = int(row_off0[j]) + NA_WIN_ROWS
            assert 0 <= lo and lo + NA_WIN_ROWS <= nrow + 2 * NA_WIN_ROWS
            blk = per_col[:, lo:lo + NA_WIN_ROWS]
            blk = jnp.where(row_valid[j][None, :, None, None], blk, NEG_BIG)
            blocks.append(jnp.transpose(blk, (0, 2, 1, 3)).reshape(C_HEADS, GRID_W, NA_WIN_ROWS * GRID_W))
        tabs.append(jnp.concatenate(blocks, axis=1))
    return jnp.stack(tabs)


def _natten_kernel(wstart_ref, type_ref, q_ref, k_ref, v_ref, tab_ref, o_ref, *, tq, win):
    del type_ref
    i = pl.program_id(1)
    start = pl.multiple_of(wstart_ref[i], GRID_W)
    lane = lax.broadcasted_iota(jnp.int32, (tq, LANES), 1)
    low = lane < HEAD_DIM
    for pair in range(C_HEADS // 2):
        cols = slice(pair * LANES, (pair + 1) * LANES)
        qp = q_ref[:, cols]
        kp = k_ref[pl.ds(start, win), cols]
        vp = v_ref[pl.ds(start, win), cols]
        zero = jnp.zeros_like(qp)
        outs = []
        for hh in range(2):
            qz = jnp.where(low if hh == 0 else jnp.logical_not(low), qp, zero)
            s = _dot_nt(qz, kp) + tab_ref[0, 2 * pair + hh]
            m = jnp.max(s, axis=-1, keepdims=True)
            p = jnp.exp2(s - m)
            l = jnp.sum(p, axis=-1, keepdims=True)
            outs.append(_dot(p.astype(BF16), vp) / l)
        o_ref[:, cols] = jnp.where(low, outs[0], outs[1]).astype(o_ref.dtype)


def _natten(hc, tab, wstart, type_of, *, bsz, seq):
    n = bsz * seq
    tq = NA_Q_ROWS * GRID_W
    win = NA_WIN_ROWS * GRID_W
    nblk = seq // tq
    grid_spec = pltpu.PrefetchScalarGridSpec(
        num_scalar_prefetch=2,
        grid=(bsz, nblk),
        in_specs=[
            pl.BlockSpec((tq, C_WIDTH), lambda b, i, ws, ty: (b * nblk + i, 0)),
            pl.BlockSpec((seq, C_WIDTH), lambda b, i, ws, ty: (b, 1)),
            pl.BlockSpec((seq, C_WIDTH), lambda b, i, ws, ty: (b, 2)),
            pl.BlockSpec((1, C_HEADS, tq, win), lambda b, i, ws, ty: (ty[i], 0, 0, 0)),
        ],
        out_specs=pl.BlockSpec((tq, C_WIDTH), lambda b, i, ws, ty: (b * nblk + i, 0)),
    )
    return pl.pallas_call(
        functools.partial(_natten_kernel, tq=tq, win=win),
        grid_spec=grid_spec,
        out_shape=jax.ShapeDtypeStruct((n, C_WIDTH), BF16),
        compiler_params=pltpu.CompilerParams(
            dimension_semantics=("parallel", "parallel"), vmem_limit_bytes=VMEM_LIMIT_BYTES),
        name="neighbourhood_attention",
    )(wstart, type_of, hc, hc, hc, tab)


def _outffn_kernel(x_ref, oa_ref, ob_ref, oc_ref, wo_ref, wg_ref, wu_ref, wd_ref, lng_ref, lnb_ref, y_ref):
    mix = (_dot(oa_ref[...], wo_ref[0:A_WIDTH, :])
           + _dot(ob_ref[...], wo_ref[A_WIDTH:A_WIDTH + B_WIDTH, :])
           + _dot(oc_ref[...], wo_ref[A_WIDTH + B_WIDTH:MIX_WIDTH, :]))
    x1 = _layer_norm(DEEPNORM_ALPHA * x_ref[...] + mix, lng_ref[0:1, :], lnb_ref[0:1, :])
    x1b = x1.astype(BF16)
    ff = None
    off = 0
    for width in FF_CHUNKS:
        cols = slice(off, off + width)
        gate = _dot(x1b, wg_ref[:, cols])
        up = _dot(x1b, wu_ref[:, cols])
        part = _dot((jax.nn.silu(gate) * up).astype(BF16), wd_ref[cols, :])
        ff = part if ff is None else ff + part
        off += width
    y_ref[...] = _layer_norm(DEEPNORM_ALPHA * x1 + ff, lng_ref[1:2, :], lnb_ref[1:2, :])


def _outffn(x2d, oa, ob, oc, wo, wg, wu, wd, ln_g, ln_b, *, tm):
    n = x2d.shape[0]
    const = lambda i: (0, 0)
    row = lambda i: (i, 0)
    resident = functools.partial(pl.BlockSpec, index_map=const, pipeline_mode=pl.Buffered(1))
    return pl.pallas_call(
        _outffn_kernel,
        grid=(n // tm,),
        in_specs=[
            pl.BlockSpec((tm, D_MODEL), row),
            pl.BlockSpec((tm, A_WIDTH), row),
            pl.BlockSpec((tm, B_WIDTH), row),
            pl.BlockSpec((tm, C_WIDTH), row),
            resident((MIX_WIDTH, D_MODEL)),
            resident((D_MODEL, D_FF)),
            resident((D_MODEL, D_FF)),
            resident((D_FF, D_MODEL)),
            pl.BlockSpec((2, D_MODEL), const),
            pl.BlockSpec((2, D_MODEL), const),
        ],
        out_specs=pl.BlockSpec((tm, D_MODEL), row),
        out_shape=jax.ShapeDtypeStruct((n, D_MODEL), F32),
        compiler_params=pltpu.CompilerParams(
            dimension_semantics=("parallel",), vmem_limit_bytes=VMEM_LIMIT_BYTES),
        name="outproj_ffn",
    )(x2d, oa, ob, oc, wo, wg, wu, wd, ln_g, ln_b)


def _prepare_layer(l, w_in, w_out, a_ln_g, a_ln_b, a_w_s, a_b_s, c_rpb, w_gate, w_up, w_down):
    col_scale = np.ones((IN_WIDTH,), np.float32)
    col_scale[OFF_B:OFF_B + B_WIDTH] = QK_SCALE * LOG2E
    col_scale[OFF_C:OFF_C + C_WIDTH] = QK_SCALE * LOG2E
    return dict(
        w_in=(w_in[l] * col_scale).astype(BF16),
        w_out=w_out[l].astype(BF16),
        a_ln_g=a_ln_g[l][None, :],
        a_ln_b=a_ln_b[l][None, :],
        ws_all=a_w_s[l].reshape(A_GROUPS * CHUNK, CHUNK).astype(BF16),
        bs_tab=jnp.repeat(jnp.transpose(a_b_s[l]), HEAD_DIM, axis=1),
        rpb=c_rpb[l],
        w_gate=w_gate[l].astype(BF16),
        w_up=w_up[l].astype(BF16),
        w_down=w_down[l].astype(BF16),
    )


def _trunk(x, layers, lam_params, subln_g, ln_g, ln_b, slopes):
    bsz, seq, _ = x.shape
    rows = seq // GRID_W
    wstart, type_of, types, col_valid = _na_tables(rows)
    x2d = x.reshape(bsz * seq, D_MODEL)
    for l, p in enumerate(layers):
        lam_init = 0.8 - 0.6 * math.exp(-0.3 * l)
        oa, hb, hc = _inproj(x2d, p["w_in"], p["a_ln_g"], p["a_ln_b"], p["ws_all"], p["bs_tab"], tm=512)
        lq1, lk1, lq2, lk2 = (a[l][None, :] for a in lam_params)
        ob = _diffattn(hb, slopes, lq1, lk1, lq2, lk2, subln_g[l][None, :],
                       bsz=bsz, seq=seq, lam_init=lam_init, tq=512, tk=512)
        tab = _na_bias_table(p["rpb"], types, col_valid)
        oc = _natten(hc, tab, jnp.asarray(wstart), jnp.asarray(type_of), bsz=bsz, seq=seq)
        x2d = _outffn(x2d, oa, ob, oc, p["w_out"], p["w_gate"], p["w_up"], p["w_down"],
                      ln_g[l], ln_b[l], tm=512)
    return x2d.reshape(bsz, seq, D_MODEL)


def kernel(x_prompt, x_sample, w_in, w_out, a_ln_g, a_ln_b, a_w_s, a_b_s, b_lambda_q1, b_lambda_k1,
           b_lambda_q2, b_lambda_k2, b_subln_g, c_rpb, w_gate, w_up, w_down, ln_g, ln_b):
    layers = [_prepare_layer(l, w_in, w_out, a_ln_g, a_ln_b, a_w_s, a_b_s, c_rpb, w_gate, w_up, w_down)
              for l in range(DEPTH)]
    lam_params = (b_lambda_q1, b_lambda_k1, b_lambda_q2, b_lambda_k2)
    slopes = jnp.exp2(-8.0 * jnp.arange(1, B_HEADS + 1, dtype=F32) / B_HEADS)
    y_prompt = _trunk(x_prompt, layers, lam_params, b_subln_g, ln_g, ln_b, slopes)
    y_sample = _trunk(x_sample, layers, lam_params, b_subln_g, ln_g, ln_b, slopes)
    return (y_prompt, y_sample)
```

```python
import functools
import math

import jax
import jax.numpy as jnp
import numpy as np
from jax import lax
from jax.experimental import pallas as pl
from jax.experimental.pallas import tpu as pltpu

F32 = jnp.float32
BF16 = jnp.bfloat16

D_MODEL = 1024
DEPTH = 2
HEAD_DIM = 64
A_GROUPS = 4
A_WIDTH = A_GROUPS * HEAD_DIM
CHUNK = 128
B_HEADS = 4
B_WIDTH = B_HEADS * 2 * HEAD_DIM
C_HEADS = 4
C_WIDTH = C_HEADS * HEAD_DIM
MIX_WIDTH = A_WIDTH + B_WIDTH + C_WIDTH
IN_WIDTH = 2 * A_WIDTH + 3 * B_WIDTH + 3 * C_WIDTH
GRID_W = 64
NA_ROWS_MAX = 8
NA_COLS = 16
D_FF = 2816
DEEPNORM_ALPHA = (2 * DEPTH) ** 0.25
LN_EPS = 1e-5
QK_SCALE = HEAD_DIM ** -0.5
LOG2E = math.log2(math.e)

LANES = 128
MXU_TILE = 256
SCORE_BUFFERS = 2
VMEM_LIMIT_BYTES = 56 * 1024 * 1024

OFF_AU, OFF_AV = 0, A_WIDTH
OFF_B = 2 * A_WIDTH
OFF_C = OFF_B + 3 * B_WIDTH

NA_Q_ROWS = 4
NA_WIN_ROWS = 12
NEG_BIG = -1e30

FF_CHUNKS = (768, 768, 768, 512)
assert sum(FF_CHUNKS) == D_FF


def _layer_norm(x, g, b):
    mu = jnp.mean(x, axis=-1, keepdims=True)
    xc = x - mu
    var = jnp.mean(xc * xc, axis=-1, keepdims=True)
    return xc * lax.rsqrt(var + LN_EPS) * g + b


def _dot(a, b):
    return jnp.dot(a, b, preferred_element_type=F32)


def _dot_nt(a, b):
    return lax.dot_general(a, b, (((1,), (1,)), ((), ())), preferred_element_type=F32)


def _inproj_kernel(x_ref, w_ref, lng_ref, lnb_ref, ws_ref, bs_ref, oa_ref, hb_ref, hc_ref, *, tm):
    xb = x_ref[...].astype(BF16)
    ha = jax.nn.gelu(_dot(xb, w_ref[:, 0:OFF_B]))
    hb_ref[...] = _dot(xb, w_ref[:, OFF_B:OFF_C]).astype(BF16)
    hc_ref[...] = _dot(xb, w_ref[:, OFF_C:IN_WIDTH]).astype(BF16)

    u = ha[:, :A_WIDTH]
    v = _layer_norm(ha[:, A_WIDTH:], lng_ref[...], lnb_ref[...]).astype(BF16)
    lane = lax.broadcasted_iota(jnp.int32, (CHUNK, A_WIDTH), 1)
    ws = ws_ref[...]
    for n in range(tm // CHUNK):
        rows = slice(n * CHUNK, (n + 1) * CHUNK)
        res = _dot(ws, v[rows, :])
        mixed = res[0:CHUNK]
        for g in range(1, A_GROUPS):
            mixed = jnp.where(lane >= g * HEAD_DIM, res[g * CHUNK:(g + 1) * CHUNK], mixed)
        oa_ref[rows, :] = (u[rows, :] * (mixed + bs_ref[...])).astype(BF16)


def _inproj(x2d, w_in_b, a_ln_g, a_ln_b, ws_all, bs_tab, *, tm):
    n = x2d.shape[0]
    const = lambda i: (0, 0)
    return pl.pallas_call(
        functools.partial(_inproj_kernel, tm=tm),
        grid=(n // tm,),
        in_specs=[
            pl.BlockSpec((tm, D_MODEL), lambda i: (i, 0)),
            pl.BlockSpec((D_MODEL, IN_WIDTH), const),
            pl.BlockSpec((1, A_WIDTH), const),
            pl.BlockSpec((1, A_WIDTH), const),
            pl.BlockSpec((A_GROUPS * CHUNK, CHUNK), const),
            pl.BlockSpec((CHUNK, A_WIDTH), const),
        ],
        out_specs=[
            pl.BlockSpec((tm, A_WIDTH), lambda i: (i, 0)),
            pl.BlockSpec((tm, 3 * B_WIDTH), lambda i: (i, 0)),
            pl.BlockSpec((tm, 3 * C_WIDTH), lambda i: (i, 0)),
        ],
        out_shape=[
            jax.ShapeDtypeStruct((n, A_WIDTH), BF16),
            jax.ShapeDtypeStruct((n, 3 * B_WIDTH), BF16),
            jax.ShapeDtypeStruct((n, 3 * C_WIDTH), BF16),
        ],
        compiler_params=pltpu.CompilerParams(
            dimension_semantics=("parallel",), vmem_limit_bytes=VMEM_LIMIT_BYTES),
        name="inproj_gating",
    )(x2d, w_in_b, a_ln_g, a_ln_b, ws_all, bs_tab)


def _split3(x):
    hi = x.astype(BF16).astype(F32)
    r = x - hi
    mid = r.astype(BF16).astype(F32)
    return hi, mid, r - mid


def _aug_lanes(x, lane, value_first, one):
    l6 = jnp.bitwise_and(lane, HEAD_DIM - 1)
    hi, mid, lo = _split3(x)
    v0 = 0 if value_first else 3
    out = jnp.where(l6 == v0, hi, jnp.where(l6 == v0 + 1, mid, jnp.where(l6 == v0 + 2, lo, 0.0)))
    ones = (l6 >= 3 - v0) & (l6 < 6 - v0)
    return jnp.where(ones, one, out)


def _diffattn_kernel(slopes_ref, q_ref, k_ref, v_ref, lq1_ref, lk1_ref, lq2_ref, lk2_ref, g_ref,
                     o_ref, ka0_ref, ka1_ref, va_ref, acc0_ref, acc1_ref, s_ref, mx_ref,
                     *, tq, tk, seq, lam_init):
    h = pl.program_id(1)
    qi = pl.program_id(2)
    c = slopes_ref[h] * LOG2E
    fill = 512
    lane_f = lax.broadcasted_iota(jnp.int32, (fill, LANES), 1)
    row_f = lax.broadcasted_iota(jnp.int32, (fill, LANES), 0)

    @pl.when(qi == 0)
    def _():
        def fill_body(r, carry):
            start = pl.multiple_of(r * fill, fill)
            pos = (row_f + r * fill).astype(F32)
            aug = _aug_lanes(c * pos, lane_f, False, 1.0).astype(BF16)
            kk = k_ref[pl.ds(start, fill), :]
            ka0_ref[pl.ds(start, fill), :] = jnp.where(lane_f < HEAD_DIM, kk, aug)
            ka1_ref[pl.ds(start, fill), :] = jnp.where(lane_f < HEAD_DIM, aug, kk)
            va_ref[pl.ds(start, fill), 0:LANES] = v_ref[pl.ds(start, fill), :]
            va_ref[pl.ds(start, fill), LANES:2 * LANES] = jnp.ones((fill, LANES), BF16)
            return carry
        lax.fori_loop(0, seq // fill, fill_body, 0)

    q = q_ref[...]
    lane = lax.broadcasted_iota(jnp.int32, (tq, LANES), 1)
    low = lane < HEAD_DIM
    t = (lax.broadcasted_iota(jnp.int32, (tq, LANES), 0) + qi * tq).astype(F32)
    aug_before = _aug_lanes(-(c * t), lane, True, 1.0)
    zero = jnp.zeros_like(q)

    def variants(aug):
        return jnp.where(low, q, aug), jnp.where(low, aug, q)

    q_before = variants(aug_before.astype(BF16))
    q_after = variants((-aug_before).astype(BF16))
    q_diag = variants(zero)

    acc0_ref[...] = jnp.zeros_like(acc0_ref)
    acc1_ref[...] = jnp.zeros_like(acc1_ref)

    nkv = seq // tk
    jd = (qi * tq) // tk
    ksub = tk // MXU_TILE

    def kv_start(i):
        j = i + (i >= jd).astype(jnp.int32)
        return j, pl.multiple_of(j * tk, tk)

    def scores_into(buf, start, qv, bias=None):
        for c2, ka_ref in enumerate((ka0_ref, ka1_ref)):
            mx = None
            for u in range(ksub):
                keys = slice(u * MXU_TILE, (u + 1) * MXU_TILE)
                s = _dot_nt(qv[c2], ka_ref[pl.ds(start + u * MXU_TILE, MXU_TILE), :])
                if bias is not None:
                    s = s - bias[:, keys]
                s_ref[buf, c2, :, keys] = s
                for n in range(MXU_TILE // LANES):
                    tile = s[:, n * LANES:(n + 1) * LANES]
                    mx = tile if mx is None else jnp.maximum(mx, tile)
            mx_ref[buf, c2] = mx

    def scores_offdiag(i, buf):
        j, start = kv_start(i)
        before = j < jd
        qv = tuple(jnp.where(before, qb, qa) for qb, qa in zip(q_before, q_after))
        scores_into(buf, start, qv)

    def step(buf, c2, m, acc_ref, start):
        m_new = jnp.maximum(m, jnp.max(mx_ref[buf, c2], axis=-1, keepdims=True))
        pv = None
        for u in range(ksub):
            keys = slice(u * MXU_TILE, (u + 1) * MXU_TILE)
            p = jnp.exp2(s_ref[buf, c2, :, keys] - m_new).astype(BF16)
            part = _dot(p, va_ref[pl.ds(start + u * MXU_TILE, MXU_TILE), :])
            pv = part if pv is None else pv + part
        acc_ref[...] = jnp.exp2(m - m_new) * acc_ref[...] + pv
        return m_new

    def consume(buf, start, carry):
        return (step(buf, 0, carry[0], acc0_ref, start), step(buf, 1, carry[1], acc1_ref, start))

    neg = jnp.full((tq, 1), -jnp.inf, F32)
    diag_start = pl.multiple_of(jd * tk, tk)
    carry = (neg, neg)
    scores_offdiag(jnp.int32(0), 0)
    for i in range(nkv):
        nxt = (i + 1) % SCORE_BUFFERS
        if i + 1 < nkv - 1:
            scores_offdiag(jnp.int32(i + 1), nxt)
        elif i + 1 == nkv - 1:
            rel = (lax.broadcasted_iota(jnp.int32, (tq, tk), 0)
                   - lax.broadcasted_iota(jnp.int32, (tq, tk), 1))
            diag_bias = c * jnp.abs(rel + (qi * tq - jd * tk)).astype(F32)
            scores_into(nxt, diag_start, q_diag, diag_bias)
        start = diag_start if i == nkv - 1 else kv_start(jnp.int32(i))[1]
        carry = consume(i % SCORE_BUFFERS, start, carry)

    lam = (jnp.exp(jnp.sum(lq1_ref[...] * lk1_ref[...], axis=-1, keepdims=True))
           - jnp.exp(jnp.sum(lq2_ref[...] * lk2_ref[...], axis=-1, keepdims=True)) + lam_init)
    o = (acc0_ref[:, 0:LANES] / acc0_ref[:, LANES:2 * LANES]
         - lam * (acc1_ref[:, 0:LANES] / acc1_ref[:, LANES:2 * LANES]))
    ms = jnp.mean(o * o, axis=-1, keepdims=True)
    o_ref[...] = (o * lax.rsqrt(ms + LN_EPS) * g_ref[...] * (1.0 - lam_init)).astype(o_ref.dtype)


def _diffattn(hb, slopes, lq1, lk1, lq2, lk2, subln_g, *, bsz, seq, lam_init, tq, tk):
    n = bsz * seq
    nq = seq // tq
    const = lambda b, h, i, sl: (0, 0)
    grid_spec = pltpu.PrefetchScalarGridSpec(
        num_scalar_prefetch=1,
        grid=(bsz, B_HEADS, nq),
        in_specs=[
            pl.BlockSpec((tq, LANES), lambda b, h, i, sl: (b * nq + i, h)),
            pl.BlockSpec((seq, LANES), lambda b, h, i, sl: (b, B_HEADS + h)),
            pl.BlockSpec((seq, LANES), lambda b, h, i, sl: (b, 2 * B_HEADS + h)),
            pl.BlockSpec((1, HEAD_DIM), const),
            pl.BlockSpec((1, HEAD_DIM), const),
            pl.BlockSpec((1, HEAD_DIM), const),
            pl.BlockSpec((1, HEAD_DIM), const),
            pl.BlockSpec((1, 2 * HEAD_DIM), const),
        ],
        out_specs=pl.BlockSpec((tq, LANES), lambda b, h, i, sl: (b * nq + i, h)),
        scratch_shapes=[pltpu.VMEM((seq, LANES), BF16), pltpu.VMEM((seq, LANES), BF16),
                        pltpu.VMEM((seq, 2 * LANES), BF16),
                        pltpu.VMEM((tq, 2 * LANES), F32), pltpu.VMEM((tq, 2 * LANES), F32),
                        pltpu.VMEM((SCORE_BUFFERS, 2, tq, tk), F32),
                        pltpu.VMEM((SCORE_BUFFERS, 2, tq, LANES), F32)],
    )
    assert seq % tk == 0 and tk % tq == 0 and tk % MXU_TILE == 0 and seq // tk >= 2
    return pl.pallas_call(
        functools.partial(_diffattn_kernel, tq=tq, tk=tk, seq=seq, lam_init=lam_init),
        grid_spec=grid_spec,
        out_shape=jax.ShapeDtypeStruct((n, B_WIDTH), BF16),
        compiler_params=pltpu.CompilerParams(
            dimension_semantics=("arbitrary", "arbitrary", "arbitrary"),
            vmem_limit_bytes=VMEM_LIMIT_BYTES),
        name="diff_attention",
    )(slopes, hb, hb, hb, lq1, lk1, lq2, lk2, subln_g)


def _na_tables(rows):
    wr = min(NA_ROWS_MAX, rows)
    nblk = rows // NA_Q_ROWS
    col = np.arange(GRID_W)
    col_start = np.clip(col - NA_COLS // 2, 0, GRID_W - NA_COLS)
    kc = np.arange(GRID_W)
    col_valid = (kc[None, :] >= col_start[:, None]) & (kc[None, :] < col_start[:, None] + NA_COLS)

    assert rows >= NA_WIN_ROWS and rows % NA_Q_ROWS == 0
    wstart, types, type_of = [], [], []
    for b in range(nblk):
        r = NA_Q_ROWS * b + np.arange(NA_Q_ROWS)
        r_start = np.clip(r - wr // 2, 0, rows - wr)
        w = int(min(r_start[0], rows - NA_WIN_ROWS))
        kr = w + np.arange(NA_WIN_ROWS)
        row_valid = (kr[None, :] >= r_start[:, None]) & (kr[None, :] < r_start[:, None] + wr)
        assert (row_valid.sum(axis=1) == wr).all()
        row_off0 = w - r + (NA_ROWS_MAX - 1)
        key = (row_off0.tobytes(), row_valid.tobytes())
        for t, (k0, _, _) in enumerate(types):
            if k0 == key:
                break
        else:
            t = len(types)
            types.append((key, row_off0, row_valid))
        wstart.append(w * GRID_W)
        type_of.append(t)
    return (np.asarray(wstart, np.int32), np.asarray(type_of, np.int32),
            [(t[1], t[2]) for t in types], col_valid)


def _na_bias_table(rpb, types, col_valid):
    nrow = rpb.shape[1]
    rp = jnp.pad(rpb.astype(F32) * LOG2E, ((0, 0), (0, 0), (GRID_W, GRID_W)))
    per_col = jnp.stack([rp[:, :, GRID_W + NA_COLS - 1 - c: 2 * GRID_W + NA_COLS - 1 - c]
                         for c in range(GRID_W)], axis=2)
    per_col = jnp.where(col_valid[None, None], per_col, NEG_BIG)
    per_col = jnp.pad(per_col, ((0, 0), (NA_WIN_ROWS, NA_WIN_ROWS), (0, 0), (0, 0)), constant_values=NEG_BIG)
    tabs = []
    for row_off0, row_valid in types:
        blocks = []
        for j in range(NA_Q_ROWS):
            lo = int(row_off0[j]) + NA_WIN_ROWS
            assert 0 <= lo and lo + NA_WIN_ROWS <= nrow + 2 * NA_WIN_ROWS
            blk = per_col[:, lo:lo + NA_WIN_ROWS]
            blk = jnp.where(row_valid[j][None, :, None, None], blk, NEG_BIG)
            blocks.append(jnp.transpose(blk, (0, 2, 1, 3)).reshape(C_HEADS, GRID_W, NA_WIN_ROWS * GRID_W))
        tabs.append(jnp.concatenate(blocks, axis=1))
    return jnp.stack(tabs)


def _natten_kernel(wstart_ref, type_ref, q_ref, k_ref, v_ref, tab_ref, o_ref, *, tq, win):
    del type_ref
    i = pl.program_id(1)
    start = pl.multiple_of(wstart_ref[i], GRID_W)
    lane = lax.broadcasted_iota(jnp.int32, (tq, LANES), 1)
    low = lane < HEAD_DIM
    for pair in range(C_HEADS // 2):
        cols = slice(pair * LANES, (pair + 1) * LANES)
        qp = q_ref[:, cols]
        kp = k_ref[pl.ds(start, win), cols]
        vp = jnp.concatenate([v_ref[pl.ds(start, win), cols], jnp.ones((win, LANES), BF16)], axis=1)
        zero = jnp.zeros_like(qp)
        outs = []
        for hh in range(2):
            qz = jnp.where(low if hh == 0 else jnp.logical_not(low), qp, zero)
            s = _dot_nt(qz, kp) + tab_ref[0, 2 * pair + hh]
            m = jnp.max(s, axis=-1, keepdims=True)
            p = jnp.exp2(s - m)
            pv = _dot(p.astype(BF16), vp)
            outs.append(pv[:, 0:LANES] / pv[:, LANES:2 * LANES])
        o_ref[:, cols] = jnp.where(low, outs[0], outs[1]).astype(o_ref.dtype)


def _natten(hc, tab, wstart, type_of, *, bsz, seq):
    n = bsz * seq
    tq = NA_Q_ROWS * GRID_W
    win = NA_WIN_ROWS * GRID_W
    nblk = seq // tq
    grid_spec = pltpu.PrefetchScalarGridSpec(
        num_scalar_prefetch=2,
        grid=(bsz, nblk),
        in_specs=[
            pl.BlockSpec((tq, C_WIDTH), lambda b, i, ws, ty: (b * nblk + i, 0)),
            pl.BlockSpec((seq, C_WIDTH), lambda b, i, ws, ty: (b, 1)),
            pl.BlockSpec((seq, C_WIDTH), lambda b, i, ws, ty: (b, 2)),
            pl.BlockSpec((1, C_HEADS, tq, win), lambda b, i, ws, ty: (ty[i], 0, 0, 0)),
        ],
        out_specs=pl.BlockSpec((tq, C_WIDTH), lambda b, i, ws, ty: (b * nblk + i, 0)),
    )
    return pl.pallas_call(
        functools.partial(_natten_kernel, tq=tq, win=win),
        grid_spec=grid_spec,
        out_shape=jax.ShapeDtypeStruct((n, C_WIDTH), BF16),
        compiler_params=pltpu.CompilerParams(
            dimension_semantics=("parallel", "parallel"), vmem_limit_bytes=VMEM_LIMIT_BYTES),
        name="neighbourhood_attention",
    )(wstart, type_of, hc, hc, hc, tab)


def _outffn_kernel(x_ref, oa_ref, ob_ref, oc_ref, wo_ref, wg_ref, wu_ref, wd_ref, lng_ref, lnb_ref, y_ref):
    tm = x_ref.shape[0]
    halves = (slice(0, tm // 2), slice(tm // 2, tm))
    x1, x1b = [], []
    for rows in halves:
        mix = (_dot(oa_ref[rows, :], wo_ref[0:A_WIDTH, :])
               + _dot(ob_ref[rows, :], wo_ref[A_WIDTH:A_WIDTH + B_WIDTH, :])
               + _dot(oc_ref[rows, :], wo_ref[A_WIDTH + B_WIDTH:MIX_WIDTH, :]))
        x1.append(_layer_norm(DEEPNORM_ALPHA * x_ref[rows, :] + mix, lng_ref[0:1, :], lnb_ref[0:1, :]))
        x1b.append(x1[-1].astype(BF16))
    ff = [None, None]
    off = 0
    for width in FF_CHUNKS:
        cols = slice(off, off + width)
        for i in range(2):
            gate = _dot(x1b[i], wg_ref[:, cols])
            up = _dot(x1b[i], wu_ref[:, cols])
            part = _dot((jax.nn.silu(gate) * up).astype(BF16), wd_ref[cols, :])
            ff[i] = part if ff[i] is None else ff[i] + part
        off += width
    for i, rows in enumerate(halves):
        y_ref[rows, :] = _layer_norm(DEEPNORM_ALPHA * x1[i] + ff[i], lng_ref[1:2, :], lnb_ref[1:2, :])


def _outffn(x2d, oa, ob, oc, wo, wg, wu, wd, ln_g, ln_b, *, tm):
    n = x2d.shape[0]
    const = lambda i: (0, 0)
    row = lambda i: (i, 0)
    resident = functools.partial(pl.BlockSpec, index_map=const, pipeline_mode=pl.Buffered(1))
    return pl.pallas_call(
        _outffn_kernel,
        grid=(n // tm,),
        in_specs=[
            pl.BlockSpec((tm, D_MODEL), row),
            pl.BlockSpec((tm, A_WIDTH), row),
            pl.BlockSpec((tm, B_WIDTH), row),
            pl.BlockSpec((tm, C_WIDTH), row),
            resident((MIX_WIDTH, D_MODEL)),
            resident((D_MODEL, D_FF)),
            resident((D_MODEL, D_FF)),
            resident((D_FF, D_MODEL)),
            pl.BlockSpec((2, D_MODEL), const),
            pl.BlockSpec((2, D_MODEL), const),
        ],
        out_specs=pl.BlockSpec((tm, D_MODEL), row),
        out_shape=jax.ShapeDtypeStruct((n, D_MODEL), F32),
        compiler_params=pltpu.CompilerParams(
            dimension_semantics=("parallel",), vmem_limit_bytes=VMEM_LIMIT_BYTES),
        name="outproj_ffn",
    )(x2d, oa, ob, oc, wo, wg, wu, wd, ln_g, ln_b)


def _prepare_layer(l, w_in, w_out, a_ln_g, a_ln_b, a_w_s, a_b_s, c_rpb, w_gate, w_up, w_down):
    col_scale = np.ones((IN_WIDTH,), np.float32)
    col_scale[OFF_B:OFF_B + B_WIDTH] = QK_SCALE * LOG2E
    col_scale[OFF_C:OFF_C + C_WIDTH] = QK_SCALE * LOG2E
    return dict(
        w_in=(w_in[l] * col_scale).astype(BF16),
        w_out=w_out[l].astype(BF16),
        a_ln_g=a_ln_g[l][None, :],
        a_ln_b=a_ln_b[l][None, :],
        ws_all=a_w_s[l].reshape(A_GROUPS * CHUNK, CHUNK).astype(BF16),
        bs_tab=jnp.repeat(jnp.transpose(a_b_s[l]), HEAD_DIM, axis=1),
        rpb=c_rpb[l],
        w_gate=w_gate[l].astype(BF16),
        w_up=w_up[l].astype(BF16),
        w_down=w_down[l].astype(BF16),
    )


def _trunk(x, layers, lam_params, subln_g, ln_g, ln_b, slopes):
    bsz, seq, _ = x.shape
    rows = seq // GRID_W
    wstart, type_of, types, col_valid = _na_tables(rows)
    x2d = x.reshape(bsz * seq, D_MODEL)
    for l, p in enumerate(layers):
        lam_init = 0.8 - 0.6 * math.exp(-0.3 * l)
        oa, hb, hc = _inproj(x2d, p["w_in"], p["a_ln_g"], p["a_ln_b"], p["ws_all"], p["bs_tab"], tm=512)
        lq1, lk1, lq2, lk2 = (a[l][None, :] for a in lam_params)
        ob = _diffattn(hb, slopes, lq1, lk1, lq2, lk2, subln_g[l][None, :],
                       bsz=bsz, seq=seq, lam_init=lam_init, tq=512, tk=512)
        tab = _na_bias_table(p["rpb"], types, col_valid)
        oc = _natten(hc, tab, jnp.asarray(wstart), jnp.asarray(type_of), bsz=bsz, seq=seq)
        x2d = _outffn(x2d, oa, ob, oc, p["w_out"], p["w_gate"], p["w_up"], p["w_down"],
                      ln_g[l], ln_b[l], tm=1024)
    return x2d.reshape(bsz, seq, D_MODEL)


def kernel(x_prompt, x_sample, w_in, w_out, a_ln_g, a_ln_b, a_w_s, a_b_s, b_lambda_q1, b_lambda_k1,
           b_lambda_q2, b_lambda_k2, b_subln_g, c_rpb, w_gate, w_up, w_down, ln_g, ln_b):
    layers = [_prepare_layer(l, w_in, w_out, a_ln_g, a_ln_b, a_w_s, a_b_s, c_rpb, w_gate, w_up, w_down)
              for l in range(DEPTH)]
    lam_params = (b_lambda_q1, b_lambda_k1, b_lambda_q2, b_lambda_k2)
    slopes = jnp.exp2(-8.0 * jnp.arange(1, B_HEADS + 1, dtype=F32) / B_HEADS)
    y_prompt = _trunk(x_prompt, layers, lam_params, b_subln_g, ln_g, ln_b, slopes)
    y_sample = _trunk(x_sample, layers, lam_params, b_subln_g, ln_g, ln_b, slopes)
    return (y_prompt, y_sample)
```

```python
import functools
import math

import jax
import jax.numpy as jnp
import numpy as np
from jax import lax
from jax.experimental import pallas as pl
from jax.experimental.pallas import tpu as pltpu

F32 = jnp.float32
BF16 = jnp.bfloat16

D_MODEL = 1024
DEPTH = 2
HEAD_DIM = 64
A_GROUPS = 4
A_WIDTH = A_GROUPS * HEAD_DIM
CHUNK = 128
B_HEADS = 4
B_WIDTH = B_HEADS * 2 * HEAD_DIM
C_HEADS = 4
C_WIDTH = C_HEADS * HEAD_DIM
MIX_WIDTH = A_WIDTH + B_WIDTH + C_WIDTH
IN_WIDTH = 2 * A_WIDTH + 3 * B_WIDTH + 3 * C_WIDTH
GRID_W = 64
NA_ROWS_MAX = 8
NA_COLS = 16
D_FF = 2816
DEEPNORM_ALPHA = (2 * DEPTH) ** 0.25
LN_EPS = 1e-5
QK_SCALE = HEAD_DIM ** -0.5
LOG2E = math.log2(math.e)

LANES = 128
MXU_TILE = 256
SCORE_BUFFERS = 2
VMEM_LIMIT_BYTES = 56 * 1024 * 1024

OFF_AU, OFF_AV = 0, A_WIDTH
OFF_B = 2 * A_WIDTH
OFF_C = OFF_B + 3 * B_WIDTH

NA_Q_ROWS = 4
NA_WIN_ROWS = 12
NEG_BIG = -1e30

SKIP_HEADS = 2
SKIP_MARGIN = 160.0

FF_CHUNKS = (768, 768, 768, 512)
assert sum(FF_CHUNKS) == D_FF


def _layer_norm(x, g, b):
    mu = jnp.mean(x, axis=-1, keepdims=True)
    xc = x - mu
    var = jnp.mean(xc * xc, axis=-1, keepdims=True)
    return xc * lax.rsqrt(var + LN_EPS) * g + b


def _dot(a, b):
    return jnp.dot(a, b, preferred_element_type=F32)


def _dot_nt(a, b):
    return lax.dot_general(a, b, (((1,), (1,)), ((), ())), preferred_element_type=F32)


def _inproj_kernel(x_ref, w_ref, lng_ref, lnb_ref, ws_ref, bs_ref, oa_ref, hb_ref, hc_ref, *, tm):
    xb = x_ref[...].astype(BF16)
    ha = jax.nn.gelu(_dot(xb, w_ref[:, 0:OFF_B]))
    hb_ref[...] = _dot(xb, w_ref[:, OFF_B:OFF_C]).astype(BF16)
    hc_ref[...] = _dot(xb, w_ref[:, OFF_C:IN_WIDTH]).astype(BF16)

    u = ha[:, :A_WIDTH]
    v = _layer_norm(ha[:, A_WIDTH:], lng_ref[...], lnb_ref[...]).astype(BF16)
    lane = lax.broadcasted_iota(jnp.int32, (CHUNK, A_WIDTH), 1)
    ws = ws_ref[...]
    for n in range(tm // CHUNK):
        rows = slice(n * CHUNK, (n + 1) * CHUNK)
        res = _dot(ws, v[rows, :])
        mixed = res[0:CHUNK]
        for g in range(1, A_GROUPS):
            mixed = jnp.where(lane >= g * HEAD_DIM, res[g * CHUNK:(g + 1) * CHUNK], mixed)
        oa_ref[rows, :] = (u[rows, :] * (mixed + bs_ref[...])).astype(BF16)


def _inproj(x2d, w_in_b, a_ln_g, a_ln_b, ws_all, bs_tab, *, tm):
    n = x2d.shape[0]
    const = lambda i: (0, 0)
    return pl.pallas_call(
        functools.partial(_inproj_kernel, tm=tm),
        grid=(n // tm,),
        in_specs=[
            pl.BlockSpec((tm, D_MODEL), lambda i: (i, 0)),
            pl.BlockSpec((D_MODEL, IN_WIDTH), const),
            pl.BlockSpec((1, A_WIDTH), const),
            pl.BlockSpec((1, A_WIDTH), const),
            pl.BlockSpec((A_GROUPS * CHUNK, CHUNK), const),
            pl.BlockSpec((CHUNK, A_WIDTH), const),
        ],
        out_specs=[
            pl.BlockSpec((tm, A_WIDTH), lambda i: (i, 0)),
            pl.BlockSpec((tm, 3 * B_WIDTH), lambda i: (i, 0)),
            pl.BlockSpec((tm, 3 * C_WIDTH), lambda i: (i, 0)),
        ],
        out_shape=[
            jax.ShapeDtypeStruct((n, A_WIDTH), BF16),
            jax.ShapeDtypeStruct((n, 3 * B_WIDTH), BF16),
            jax.ShapeDtypeStruct((n, 3 * C_WIDTH), BF16),
        ],
        compiler_params=pltpu.CompilerParams(
            dimension_semantics=("parallel",), vmem_limit_bytes=VMEM_LIMIT_BYTES),
        name="inproj_gating",
    )(x2d, w_in_b, a_ln_g, a_ln_b, ws_all, bs_tab)


def _split3(x):
    hi = x.astype(BF16).astype(F32)
    r = x - hi
    mid = r.astype(BF16).astype(F32)
    return hi, mid, r - mid


def _aug_lanes(x, lane, value_first, one):
    l6 = jnp.bitwise_and(lane, HEAD_DIM - 1)
    hi, mid, lo = _split3(x)
    v0 = 0 if value_first else 3
    out = jnp.where(l6 == v0, hi, jnp.where(l6 == v0 + 1, mid, jnp.where(l6 == v0 + 2, lo, 0.0)))
    ones = (l6 >= 3 - v0) & (l6 < 6 - v0)
    return jnp.where(ones, one, out)


def _diffattn_kernel(slopes_ref, q_ref, k_ref, v_ref, lq1_ref, lk1_ref, lq2_ref, lk2_ref, g_ref,
                     o_ref, ka0_ref, ka1_ref, va_ref, acc0_ref, acc1_ref, s_ref, mx_ref, kn_ref,
                     *, tq, tk, seq, lam_init, head0, skip):
    h = pl.program_id(1) + head0
    qi = pl.program_id(2)
    c = slopes_ref[h] * LOG2E
    fill = 512
    lane_f = lax.broadcasted_iota(jnp.int32, (fill, LANES), 1)
    row_f = lax.broadcasted_iota(jnp.int32, (fill, LANES), 0)

    @pl.when(qi == 0)
    def _():
        def fill_body(r, carry):
            start = pl.multiple_of(r * fill, fill)
            pos = (row_f + r * fill).astype(F32)
            aug = _aug_lanes(c * pos, lane_f, False, 1.0).astype(BF16)
            kk = k_ref[pl.ds(start, fill), :]
            ka0_ref[pl.ds(start, fill), :] = jnp.where(lane_f < HEAD_DIM, kk, aug)
            ka1_ref[pl.ds(start, fill), :] = jnp.where(lane_f < HEAD_DIM, aug, kk)
            va_ref[pl.ds(start, fill), 0:LANES] = v_ref[pl.ds(start, fill), :]
            va_ref[pl.ds(start, fill), LANES:2 * LANES] = jnp.ones((fill, LANES), BF16)
            if not skip:
                return carry
            sq = jnp.square(kk.astype(F32))
            n0 = jnp.sum(jnp.where(lane_f < HEAD_DIM, sq, 0.0), axis=-1, keepdims=True)
            n1 = jnp.sum(jnp.where(lane_f < HEAD_DIM, 0.0, sq), axis=-1, keepdims=True)
            return jnp.maximum(carry[0], n0), jnp.maximum(carry[1], n1)
        init = (jnp.zeros((fill, 1), F32),) * 2 if skip else 0
        norms = lax.fori_loop(0, seq // fill, fill_body, init)
        if skip:
            for c2 in range(2):
                kn_ref[c2] = jnp.broadcast_to(jnp.max(norms[c2], axis=0, keepdims=True), (8, LANES))

    q = q_ref[...]
    lane = lax.broadcasted_iota(jnp.int32, (tq, LANES), 1)
    low = lane < HEAD_DIM
    t = (lax.broadcasted_iota(jnp.int32, (tq, LANES), 0) + qi * tq).astype(F32)
    aug_before = _aug_lanes(-(c * t), lane, True, 1.0)
    zero = jnp.zeros_like(q)

    def variants(aug):
        return jnp.where(low, q, aug), jnp.where(low, aug, q)

    q_before = variants(aug_before.astype(BF16))
    q_after = variants((-aug_before).astype(BF16))
    q_diag = variants(zero)

    acc0_ref[...] = jnp.zeros_like(acc0_ref)
    acc1_ref[...] = jnp.zeros_like(acc1_ref)

    nkv = seq // tk
    jd = (qi * tq) // tk
    ksub = tk // MXU_TILE

    def kv_start(i):
        j = i + (i >= jd).astype(jnp.int32)
        return j, pl.multiple_of(j * tk, tk)

    def scores_into(buf, start, qv, bias=None):
        for c2, ka_ref in enumerate((ka0_ref, ka1_ref)):
            mx = None
            for u in range(ksub):
                keys = slice(u * MXU_TILE, (u + 1) * MXU_TILE)
                s = _dot_nt(qv[c2], ka_ref[pl.ds(start + u * MXU_TILE, MXU_TILE), :])
                if bias is not None:
                    s = s - bias[:, keys]
                s_ref[buf, c2, :, keys] = s
                for n in range(MXU_TILE // LANES):
                    tile = s[:, n * LANES:(n + 1) * LANES]
                    mx = tile if mx is None else jnp.maximum(mx, tile)
            mx_ref[buf, c2] = mx

    def scores_offdiag(i, buf):
        j, start = kv_start(i)
        before = j < jd
        qv = tuple(jnp.where(before, qb, qa) for qb, qa in zip(q_before, q_after))
        scores_into(buf, start, qv)

    def step(buf, c2, m, acc_ref, start):
        m_new = jnp.maximum(m, jnp.max(mx_ref[buf, c2], axis=-1, keepdims=True))
        pv = None
        for u in range(ksub):
            keys = slice(u * MXU_TILE, (u + 1) * MXU_TILE)
            p = jnp.exp2(s_ref[buf, c2, :, keys] - m_new).astype(BF16)
            part = _dot(p, va_ref[pl.ds(start + u * MXU_TILE, MXU_TILE), :])
            pv = part if pv is None else pv + part
        acc_ref[...] = jnp.exp2(m - m_new) * acc_ref[...] + pv
        return m_new

    def consume(buf, start, carry):
        return (step(buf, 0, carry[0], acc0_ref, start), step(buf, 1, carry[1], acc1_ref, start))

    neg = jnp.full((tq, 1), -jnp.inf, F32)
    diag_start = pl.multiple_of(jd * tk, tk)
    rel = (lax.broadcasted_iota(jnp.int32, (tq, tk), 0)
           - lax.broadcasted_iota(jnp.int32, (tq, tk), 1))
    diag_bias = c * jnp.abs(rel + (qi * tq - jd * tk)).astype(F32)
    carry = (neg, neg)
    if not skip:
        scores_offdiag(jnp.int32(0), 0)
        for i in range(nkv):
            nxt = (i + 1) % SCORE_BUFFERS
            if i + 1 < nkv - 1:
                scores_offdiag(jnp.int32(i + 1), nxt)
            elif i + 1 == nkv - 1:
                scores_into(nxt, diag_start, q_diag, diag_bias)
            start = diag_start if i == nkv - 1 else kv_start(jnp.int32(i))[1]
            carry = consume(i % SCORE_BUFFERS, start, carry)
    else:
        def near_start(i, nb):
            j = jnp.where(i < nb, jd - 1 - i, jd + 1 + (i - nb))
            j = jnp.clip(j, 0, nkv - 1)
            return j, pl.multiple_of(j * tk, tk)

        def scores_near(i, nb, buf):
            j, start = near_start(i, nb)
            before = j < jd
            qv = tuple(jnp.where(before, qb, qa) for qb, qa in zip(q_before, q_after))
            scores_into(buf, start, qv)

        scores_into(0, diag_start, q_diag, diag_bias)
        scores_near(jnp.int32(0), jnp.minimum(jd, 1), 1)
        carry = consume(0, diag_start, carry)

        sq = jnp.square(q.astype(F32))
        q2 = (jnp.max(jnp.sum(jnp.where(low, sq, 0.0), axis=-1, keepdims=True), axis=0, keepdims=True),
              jnp.max(jnp.sum(jnp.where(low, 0.0, sq), axis=-1, keepdims=True), axis=0, keepdims=True))
        ub = jnp.maximum(jnp.sqrt(q2[0] * kn_ref[0][0:1, 0:1]), jnp.sqrt(q2[1] * kn_ref[1][0:1, 0:1]))
        ub = ub * 1.001 + 0.5
        m_min = jnp.min(jnp.minimum(carry[0], carry[1]), axis=0, keepdims=True)
        d = lax.broadcasted_iota(jnp.int32, (1, LANES), 1) + 1
        reach = ub - c * ((d - 1) * tk + 1).astype(F32) > m_min - SKIP_MARGIN
        needed = jnp.where(reach & (d <= nkv - 1), 1, 0)
        dist = jnp.sum(needed, axis=-1, keepdims=True)[0, 0]
        nb = jnp.minimum(dist, jd)
        n_off = nb + jnp.minimum(dist, nkv - 1 - jd)

        def pair(t, carry):
            i0 = 2 * t
            scores_near(i0 + 1, nb, 0)
            carry = consume(1, near_start(i0, nb)[1], carry)
            scores_near(i0 + 2, nb, 1)
            return consume(0, near_start(i0 + 1, nb)[1], carry)

        carry = lax.fori_loop(0, n_off // 2, pair, carry)

        @pl.when(n_off % 2 == 1)
        def _():
            consume(1, near_start(n_off - 1, nb)[1], carry)

    lam = (jnp.exp(jnp.sum(lq1_ref[...] * lk1_ref[...], axis=-1, keepdims=True))
           - jnp.exp(jnp.sum(lq2_ref[...] * lk2_ref[...], axis=-1, keepdims=True)) + lam_init)
    o = (acc0_ref[:, 0:LANES] / acc0_ref[:, LANES:2 * LANES]
         - lam * (acc1_ref[:, 0:LANES] / acc1_ref[:, LANES:2 * LANES]))
    ms = jnp.mean(o * o, axis=-1, keepdims=True)
    o_ref[...] = (o * lax.rsqrt(ms + LN_EPS) * g_ref[...] * (1.0 - lam_init)).astype(o_ref.dtype)


def _diffattn(hb, slopes, lq1, lk1, lq2, lk2, subln_g, *, bsz, seq, lam_init, tq, tk, head0, nheads, skip):
    n = bsz * seq
    nq = seq // tq
    const = lambda b, h, i, sl: (0, 0)
    grid_spec = pltpu.PrefetchScalarGridSpec(
        num_scalar_prefetch=1,
        grid=(bsz, nheads, nq),
        in_specs=[
            pl.BlockSpec((tq, LANES), lambda b, h, i, sl: (b * nq + i, head0 + h)),
            pl.BlockSpec((seq, LANES), lambda b, h, i, sl: (b, B_HEADS + head0 + h)),
            pl.BlockSpec((seq, LANES), lambda b, h, i, sl: (b, 2 * B_HEADS + head0 + h)),
            pl.BlockSpec((1, HEAD_DIM), const),
            pl.BlockSpec((1, HEAD_DIM), const),
            pl.BlockSpec((1, HEAD_DIM), const),
            pl.BlockSpec((1, HEAD_DIM), const),
            pl.BlockSpec((1, 2 * HEAD_DIM), const),
        ],
        out_specs=pl.BlockSpec((tq, LANES), lambda b, h, i, sl: (b * nq + i, h)),
        scratch_shapes=[pltpu.VMEM((seq, LANES), BF16), pltpu.VMEM((seq, LANES), BF16),
                        pltpu.VMEM((seq, 2 * LANES), BF16),
                        pltpu.VMEM((tq, 2 * LANES), F32), pltpu.VMEM((tq, 2 * LANES), F32),
                        pltpu.VMEM((SCORE_BUFFERS, 2, tq, tk), F32),
                        pltpu.VMEM((SCORE_BUFFERS, 2, tq, LANES), F32),
                        pltpu.VMEM((2, 8, LANES), F32)],
    )
    assert seq % tk == 0 and tk % tq == 0 and tk % MXU_TILE == 0 and seq // tk >= 2
    assert not skip or tq == tk
    return pl.pallas_call(
        functools.partial(_diffattn_kernel, tq=tq, tk=tk, seq=seq, lam_init=lam_init, head0=head0, skip=skip),
        grid_spec=grid_spec,
        out_shape=jax.ShapeDtypeStruct((n, nheads * LANES), BF16),
        compiler_params=pltpu.CompilerParams(
            dimension_semantics=("arbitrary", "arbitrary", "arbitrary"),
            vmem_limit_bytes=VMEM_LIMIT_BYTES),
        name="diff_attention_near" if skip else "diff_attention",
    )(slopes, hb, hb, hb, lq1, lk1, lq2, lk2, subln_g)


def _na_tables(rows):
    wr = min(NA_ROWS_MAX, rows)
    nblk = rows // NA_Q_ROWS
    col = np.arange(GRID_W)
    col_start = np.clip(col - NA_COLS // 2, 0, GRID_W - NA_COLS)
    kc = np.arange(GRID_W)
    col_valid = (kc[None, :] >= col_start[:, None]) & (kc[None, :] < col_start[:, None] + NA_COLS)

    assert rows >= NA_WIN_ROWS and rows % NA_Q_ROWS == 0
    wstart, types, type_of = [], [], []
    for b in range(nblk):
        r = NA_Q_ROWS * b + np.arange(NA_Q_ROWS)
        r_start = np.clip(r - wr // 2, 0, rows - wr)
        w = int(min(r_start[0], rows - NA_WIN_ROWS))
        kr = w + np.arange(NA_WIN_ROWS)
        row_valid = (kr[None, :] >= r_start[:, None]) & (kr[None, :] < r_start[:, None] + wr)
        assert (row_valid.sum(axis=1) == wr).all()
        row_off0 = w - r + (NA_ROWS_MAX - 1)
        key = (row_off0.tobytes(), row_valid.tobytes())
        for t, (k0, _, _) in enumerate(types):
            if k0 == key:
                break
        else:
            t = len(types)
            types.append((key, row_off0, row_valid))
        wstart.append(w * GRID_W)
        type_of.append(t)
    return (np.asarray(wstart, np.int32), np.asarray(type_of, np.int32),
            [(t[1], t[2]) for t in types], col_valid)


def _na_bias_table(rpb, types, col_valid):
    nrow = rpb.shape[1]
    rp = jnp.pad(rpb.astype(F32) * LOG2E, ((0, 0), (0, 0), (GRID_W, GRID_W)))
    per_col = jnp.stack([rp[:, :, GRID_W + NA_COLS - 1 - c: 2 * GRID_W + NA_COLS - 1 - c]
                         for c in range(GRID_W)], axis=2)
    per_col = jnp.where(col_valid[None, None], per_col, NEG_BIG)
    per_col = jnp.pad(per_col, ((0, 0), (NA_WIN_ROWS, NA_WIN_ROWS), (0, 0), (0, 0)), constant_values=NEG_BIG)
    tabs = []
    for row_off0, row_valid in types:
        blocks = []
        for j in range(NA_Q_ROWS):
            lo = int(row_off0[j]) + NA_WIN_ROWS
            assert 0 <= lo and lo + NA_WIN_ROWS <= nrow + 2 * NA_WIN_ROWS
            blk = per_col[:, lo:lo + NA_WIN_ROWS]
            blk = jnp.where(row_valid[j][None, :, None, None], blk, NEG_BIG)
            blocks.append(jnp.transpose(blk, (0, 2, 1, 3)).reshape(C_HEADS, GRID_W, NA_WIN_ROWS * GRID_W))
        tabs.append(jnp.concatenate(blocks, axis=1))
    return jnp.stack(tabs)


def _natten_kernel(wstart_ref, type_ref, q_ref, k_ref, v_ref, tab_ref, o_ref, *, tq, win):
    del type_ref
    i = pl.program_id(1)
    start = pl.multiple_of(wstart_ref[i], GRID_W)
    lane = lax.broadcasted_iota(jnp.int32, (tq, LANES), 1)
    low = lane < HEAD_DIM
    for pair in range(C_HEADS // 2):
        cols = slice(pair * LANES, (pair + 1) * LANES)
        qp = q_ref[:, cols]
        kp = k_ref[pl.ds(start, win), cols]
        vp = jnp.concatenate([v_ref[pl.ds(start, win), cols], jnp.ones((win, LANES), BF16)], axis=1)
        zero = jnp.zeros_like(qp)
        outs = []
        for hh in range(2):
            qz = jnp.where(low if hh == 0 else jnp.logical_not(low), qp, zero)
            s = _dot_nt(qz, kp) + tab_ref[0, 2 * pair + hh]
            m = jnp.max(s, axis=-1, keepdims=True)
            p = jnp.exp2(s - m)
            pv = _dot(p.astype(BF16), vp)
            outs.append(pv[:, 0:LANES] / pv[:, LANES:2 * LANES])
        o_ref[:, cols] = jnp.where(low, outs[0], outs[1]).astype(o_ref.dtype)


def _natten(hc, tab, wstart, type_of, *, bsz, seq):
    n = bsz * seq
    tq = NA_Q_ROWS * GRID_W
    win = NA_WIN_ROWS * GRID_W
    nblk = seq // tq
    grid_spec = pltpu.PrefetchScalarGridSpec(
        num_scalar_prefetch=2,
        grid=(bsz, nblk),
        in_specs=[
            pl.BlockSpec((tq, C_WIDTH), lambda b, i, ws, ty: (b * nblk + i, 0)),
            pl.BlockSpec((seq, C_WIDTH), lambda b, i, ws, ty: (b, 1)),
            pl.BlockSpec((seq, C_WIDTH), lambda b, i, ws, ty: (b, 2)),
            pl.BlockSpec((1, C_HEADS, tq, win), lambda b, i, ws, ty: (ty[i], 0, 0, 0)),
        ],
        out_specs=pl.BlockSpec((tq, C_WIDTH), lambda b, i, ws, ty: (b * nblk + i, 0)),
    )
    return pl.pallas_call(
        functools.partial(_natten_kernel, tq=tq, win=win),
        grid_spec=grid_spec,
        out_shape=jax.ShapeDtypeStruct((n, C_WIDTH), BF16),
        compiler_params=pltpu.CompilerParams(
            dimension_semantics=("parallel", "parallel"), vmem_limit_bytes=VMEM_LIMIT_BYTES),
        name="neighbourhood_attention",
    )(wstart, type_of, hc, hc, hc, tab)


def _outffn_kernel(x_ref, oa_ref, ob0_ref, ob1_ref, oc_ref, wo_ref, wg_ref, wu_ref, wd_ref, lng_ref, lnb_ref,
                   y_ref):
    tm = x_ref.shape[0]
    halves = (slice(0, tm // 2), slice(tm // 2, tm))
    x1, x1b = [], []
    for rows in halves:
        mix = (_dot(oa_ref[rows, :], wo_ref[0:A_WIDTH, :])
               + _dot(ob0_ref[rows, :], wo_ref[A_WIDTH:A_WIDTH + SKIP_HEADS * LANES, :])
               + _dot(ob1_ref[rows, :], wo_ref[A_WIDTH + SKIP_HEADS * LANES:A_WIDTH + B_WIDTH, :])
               + _dot(oc_ref[rows, :], wo_ref[A_WIDTH + B_WIDTH:MIX_WIDTH, :]))
        x1.append(_layer_norm(DEEPNORM_ALPHA * x_ref[rows, :] + mix, lng_ref[0:1, :], lnb_ref[0:1, :]))
        x1b.append(x1[-1].astype(BF16))
    ff = [None, None]
    off = 0
    for width in FF_CHUNKS:
        cols = slice(off, off + width)
        for i in range(2):
            gate = _dot(x1b[i], wg_ref[:, cols])
            up = _dot(x1b[i], wu_ref[:, cols])
            part = _dot((jax.nn.silu(gate) * up).astype(BF16), wd_ref[cols, :])
            ff[i] = part if ff[i] is None else ff[i] + part
        off += width
    for i, rows in enumerate(halves):
        y_ref[rows, :] = _layer_norm(DEEPNORM_ALPHA * x1[i] + ff[i], lng_ref[1:2, :], lnb_ref[1:2, :])


def _outffn(x2d, oa, ob, oc, wo, wg, wu, wd, ln_g, ln_b, *, tm):
    n = x2d.shape[0]
    const = lambda i: (0, 0)
    row = lambda i: (i, 0)
    resident = functools.partial(pl.BlockSpec, index_map=const, pipeline_mode=pl.Buffered(1))
    return pl.pallas_call(
        _outffn_kernel,
        grid=(n // tm,),
        in_specs=[
            pl.BlockSpec((tm, D_MODEL), row),
            pl.BlockSpec((tm, A_WIDTH), row),
            pl.BlockSpec((tm, SKIP_HEADS * LANES), row),
            pl.BlockSpec((tm, B_WIDTH - SKIP_HEADS * LANES), row),
            pl.BlockSpec((tm, C_WIDTH), row),
            resident((MIX_WIDTH, D_MODEL)),
            resident((D_MODEL, D_FF)),
            resident((D_MODEL, D_FF)),
            resident((D_FF, D_MODEL)),
            pl.BlockSpec((2, D_MODEL), const),
            pl.BlockSpec((2, D_MODEL), const),
        ],
        out_specs=pl.BlockSpec((tm, D_MODEL), row),
        out_shape=jax.ShapeDtypeStruct((n, D_MODEL), F32),
        compiler_params=pltpu.CompilerParams(
            dimension_semantics=("parallel",), vmem_limit_bytes=VMEM_LIMIT_BYTES),
        name="outproj_ffn",
    )(x2d, oa, ob[0], ob[1], oc, wo, wg, wu, wd, ln_g, ln_b)


def _prepare_layer(l, w_in, w_out, a_ln_g, a_ln_b, a_w_s, a_b_s, c_rpb, w_gate, w_up, w_down):
    col_scale = np.ones((IN_WIDTH,), np.float32)
    col_scale[OFF_B:OFF_B + B_WIDTH] = QK_SCALE * LOG2E
    col_scale[OFF_C:OFF_C + C_WIDTH] = QK_SCALE * LOG2E
    return dict(
        w_in=(w_in[l] * col_scale).astype(BF16),
        w_out=w_out[l].astype(BF16),
        a_ln_g=a_ln_g[l][None, :],
        a_ln_b=a_ln_b[l][None, :],
        ws_all=a_w_s[l].reshape(A_GROUPS * CHUNK, CHUNK).astype(BF16),
        bs_tab=jnp.repeat(jnp.transpose(a_b_s[l]), HEAD_DIM, axis=1),
        rpb=c_rpb[l],
        w_gate=w_gate[l].astype(BF16),
        w_up=w_up[l].astype(BF16),
        w_down=w_down[l].astype(BF16),
    )


def _trunk(x, layers, lam_params, subln_g, ln_g, ln_b, slopes):
    bsz, seq, _ = x.shape
    rows = seq // GRID_W
    wstart, type_of, types, col_valid = _na_tables(rows)
    x2d = x.reshape(bsz * seq, D_MODEL)
    for l, p in enumerate(layers):
        lam_init = 0.8 - 0.6 * math.exp(-0.3 * l)
        oa, hb, hc = _inproj(x2d, p["w_in"], p["a_ln_g"], p["a_ln_b"], p["ws_all"], p["bs_tab"], tm=512)
        lq1, lk1, lq2, lk2 = (a[l][None, :] for a in lam_params)
        attn = functools.partial(_diffattn, hb, slopes, lq1, lk1, lq2, lk2, subln_g[l][None, :],
                                 bsz=bsz, seq=seq, lam_init=lam_init, tq=512, tk=512)
        ob = (attn(head0=0, nheads=SKIP_HEADS, skip=True),
              attn(head0=SKIP_HEADS, nheads=B_HEADS - SKIP_HEADS, skip=False))
        tab = _na_bias_table(p["rpb"], types, col_valid)
        oc = _natten(hc, tab, jnp.asarray(wstart), jnp.asarray(type_of), bsz=bsz, seq=seq)
        x2d = _outffn(x2d, oa, ob, oc, p["w_out"], p["w_gate"], p["w_up"], p["w_down"],
                      ln_g[l], ln_b[l], tm=1024)
    return x2d.reshape(bsz, seq, D_MODEL)


def kernel(x_prompt, x_sample, w_in, w_out, a_ln_g, a_ln_b, a_w_s, a_b_s, b_lambda_q1, b_lambda_k1,
           b_lambda_q2, b_lambda_k2, b_subln_g, c_rpb, w_gate, w_up, w_down, ln_g, ln_b):
    layers = [_prepare_layer(l, w_in, w_out, a_ln_g, a_ln_b, a_w_s, a_b_s, c_rpb, w_gate, w_up, w_down)
              for l in range(DEPTH)]
    lam_params = (b_lambda_q1, b_lambda_k1, b_lambda_q2, b_lambda_k2)
    slopes = jnp.exp2(-8.0 * jnp.arange(1, B_HEADS + 1, dtype=F32) / B_HEADS)
    y_prompt = _trunk(x_prompt, layers, lam_params, b_subln_g, ln_g, ln_b, slopes)
    y_sample = _trunk(x_sample, layers, lam_params, b_subln_g, ln_g, ln_b, slopes)
    return (y_prompt, y_sample)
```

```python
import functools
import math

import jax
import jax.numpy as jnp
import numpy as np
from jax import lax
from jax.experimental import pallas as pl
from jax.experimental.pallas import tpu as pltpu

F32 = jnp.float32
BF16 = jnp.bfloat16

D_MODEL = 1024
DEPTH = 2
HEAD_DIM = 64
A_GROUPS = 4
A_WIDTH = A_GROUPS * HEAD_DIM
CHUNK = 128
B_HEADS = 4
B_WIDTH = B_HEADS * 2 * HEAD_DIM
C_HEADS = 4
C_WIDTH = C_HEADS * HEAD_DIM
MIX_WIDTH = A_WIDTH + B_WIDTH + C_WIDTH
IN_WIDTH = 2 * A_WIDTH + 3 * B_WIDTH + 3 * C_WIDTH
GRID_W = 64
NA_ROWS_MAX = 8
NA_COLS = 16
D_FF = 2816
DEEPNORM_ALPHA = (2 * DEPTH) ** 0.25
LN_EPS = 1e-5
QK_SCALE = HEAD_DIM ** -0.5
LOG2E = math.log2(math.e)

LANES = 128
MXU_TILE = 256
SCORE_BUFFERS = 2
VMEM_LIMIT_BYTES = 56 * 1024 * 1024

OFF_AU, OFF_AV = 0, A_WIDTH
OFF_B = 2 * A_WIDTH
OFF_C = OFF_B + 3 * B_WIDTH

NA_Q_ROWS = 4
NA_WIN_ROWS = 12
NEG_BIG = -1e30

SKIP_HEADS = 1
SKIP_MARGIN = 160.0

FF_CHUNKS = (768, 768, 768, 512)
assert sum(FF_CHUNKS) == D_FF


def _layer_norm(x, g, b):
    mu = jnp.mean(x, axis=-1, keepdims=True)
    xc = x - mu
    var = jnp.mean(xc * xc, axis=-1, keepdims=True)
    return xc * lax.rsqrt(var + LN_EPS) * g + b


def _dot(a, b):
    return jnp.dot(a, b, preferred_element_type=F32)


def _dot_nt(a, b):
    return lax.dot_general(a, b, (((1,), (1,)), ((), ())), preferred_element_type=F32)


def _inproj_kernel(x_ref, w_ref, lng_ref, lnb_ref, ws_ref, bs_ref, oa_ref, hb_ref, hc_ref, *, tm):
    xb = x_ref[...].astype(BF16)
    ha = jax.nn.gelu(_dot(xb, w_ref[:, 0:OFF_B]))
    hb_ref[...] = _dot(xb, w_ref[:, OFF_B:OFF_C]).astype(BF16)
    hc_ref[...] = _dot(xb, w_ref[:, OFF_C:IN_WIDTH]).astype(BF16)

    u = ha[:, :A_WIDTH]
    v = _layer_norm(ha[:, A_WIDTH:], lng_ref[...], lnb_ref[...]).astype(BF16)
    lane = lax.broadcasted_iota(jnp.int32, (CHUNK, A_WIDTH), 1)
    ws = ws_ref[...]
    for n in range(tm // CHUNK):
        rows = slice(n * CHUNK, (n + 1) * CHUNK)
        res = _dot(ws, v[rows, :])
        mixed = res[0:CHUNK]
        for g in range(1, A_GROUPS):
            mixed = jnp.where(lane >= g * HEAD_DIM, res[g * CHUNK:(g + 1) * CHUNK], mixed)
        oa_ref[rows, :] = (u[rows, :] * (mixed + bs_ref[...])).astype(BF16)


def _inproj(x2d, w_in_b, a_ln_g, a_ln_b, ws_all, bs_tab, *, tm):
    n = x2d.shape[0]
    const = lambda i: (0, 0)
    return pl.pallas_call(
        functools.partial(_inproj_kernel, tm=tm),
        grid=(n // tm,),
        in_specs=[
            pl.BlockSpec((tm, D_MODEL), lambda i: (i, 0)),
            pl.BlockSpec((D_MODEL, IN_WIDTH), const),
            pl.BlockSpec((1, A_WIDTH), const),
            pl.BlockSpec((1, A_WIDTH), const),
            pl.BlockSpec((A_GROUPS * CHUNK, CHUNK), const),
            pl.BlockSpec((CHUNK, A_WIDTH), const),
        ],
        out_specs=[
            pl.BlockSpec((tm, A_WIDTH), lambda i: (i, 0)),
            pl.BlockSpec((tm, 3 * B_WIDTH), lambda i: (i, 0)),
            pl.BlockSpec((tm, 3 * C_WIDTH), lambda i: (i, 0)),
        ],
        out_shape=[
            jax.ShapeDtypeStruct((n, A_WIDTH), BF16),
            jax.ShapeDtypeStruct((n, 3 * B_WIDTH), BF16),
            jax.ShapeDtypeStruct((n, 3 * C_WIDTH), BF16),
        ],
        compiler_params=pltpu.CompilerParams(
            dimension_semantics=("parallel",), vmem_limit_bytes=VMEM_LIMIT_BYTES),
        name="inproj_gating",
    )(x2d, w_in_b, a_ln_g, a_ln_b, ws_all, bs_tab)


def _split3(x):
    hi = x.astype(BF16).astype(F32)
    r = x - hi
    mid = r.astype(BF16).astype(F32)
    return hi, mid, r - mid


def _aug_lanes(x, lane, value_first, one):
    l6 = jnp.bitwise_and(lane, HEAD_DIM - 1)
    hi, mid, lo = _split3(x)
    v0 = 0 if value_first else 3
    out = jnp.where(l6 == v0, hi, jnp.where(l6 == v0 + 1, mid, jnp.where(l6 == v0 + 2, lo, 0.0)))
    ones = (l6 >= 3 - v0) & (l6 < 6 - v0)
    return jnp.where(ones, one, out)


def _diffattn_kernel(slopes_ref, q_ref, k_ref, v_ref, lq1_ref, lk1_ref, lq2_ref, lk2_ref, g_ref,
                     o_ref, ka0_ref, ka1_ref, va_ref, acc0_ref, acc1_ref, s_ref, mx_ref, kn_ref, db_ref,
                     *, tq, tk, seq, lam_init, head0, skip):
    h = pl.program_id(1) + head0
    qi = pl.program_id(2)
    c = slopes_ref[h] * LOG2E
    fill = 512
    lane_f = lax.broadcasted_iota(jnp.int32, (fill, LANES), 1)
    row_f = lax.broadcasted_iota(jnp.int32, (fill, LANES), 0)

    @pl.when(qi == 0)
    def _():
        def fill_body(r, carry):
            start = pl.multiple_of(r * fill, fill)
            pos = (row_f + r * fill).astype(F32)
            aug = _aug_lanes(c * pos, lane_f, False, 1.0).astype(BF16)
            kk = k_ref[pl.ds(start, fill), :]
            ka0_ref[pl.ds(start, fill), :] = jnp.where(lane_f < HEAD_DIM, kk, aug)
            ka1_ref[pl.ds(start, fill), :] = jnp.where(lane_f < HEAD_DIM, aug, kk)
            va_ref[pl.ds(start, fill), 0:LANES] = v_ref[pl.ds(start, fill), :]
            va_ref[pl.ds(start, fill), LANES:2 * LANES] = jnp.ones((fill, LANES), BF16)
            if not skip:
                return carry
            sq = jnp.square(kk.astype(F32))
            n0 = jnp.sum(jnp.where(lane_f < HEAD_DIM, sq, 0.0), axis=-1, keepdims=True)
            n1 = jnp.sum(jnp.where(lane_f < HEAD_DIM, 0.0, sq), axis=-1, keepdims=True)
            return jnp.maximum(carry[0], n0), jnp.maximum(carry[1], n1)
        init = (jnp.zeros((fill, 1), F32),) * 2 if skip else 0
        norms = lax.fori_loop(0, seq // fill, fill_body, init)
        rel = (lax.broadcasted_iota(jnp.int32, (tq, tk), 0)
               - lax.broadcasted_iota(jnp.int32, (tq, tk), 1))
        db_ref[...] = c * jnp.abs(rel).astype(F32)
        if skip:
            for c2 in range(2):
                kn_ref[c2] = jnp.broadcast_to(jnp.max(norms[c2], axis=0, keepdims=True), (8, LANES))

    q = q_ref[...]
    lane = lax.broadcasted_iota(jnp.int32, (tq, LANES), 1)
    low = lane < HEAD_DIM
    t = (lax.broadcasted_iota(jnp.int32, (tq, LANES), 0) + qi * tq).astype(F32)
    aug_before = _aug_lanes(-(c * t), lane, True, 1.0)
    zero = jnp.zeros_like(q)

    def variants(aug):
        return jnp.where(low, q, aug), jnp.where(low, aug, q)

    q_before = variants(aug_before.astype(BF16))
    q_after = variants((-aug_before).astype(BF16))
    q_diag = variants(zero)

    acc0_ref[...] = jnp.zeros_like(acc0_ref)
    acc1_ref[...] = jnp.zeros_like(acc1_ref)

    nkv = seq // tk
    jd = (qi * tq) // tk
    ksub = tk // MXU_TILE

    def kv_start(i):
        j = i + (i >= jd).astype(jnp.int32)
        return j, pl.multiple_of(j * tk, tk)

    def scores_into(buf, start, qv, bias=None):
        for c2, ka_ref in enumerate((ka0_ref, ka1_ref)):
            mx = None
            for u in range(ksub):
                keys = slice(u * MXU_TILE, (u + 1) * MXU_TILE)
                s = _dot_nt(qv[c2], ka_ref[pl.ds(start + u * MXU_TILE, MXU_TILE), :])
                if bias is not None:
                    s = s - bias[:, keys]
                s_ref[buf, c2, :, keys] = s
                for n in range(MXU_TILE // LANES):
                    tile = s[:, n * LANES:(n + 1) * LANES]
                    mx = tile if mx is None else jnp.maximum(mx, tile)
            mx_ref[buf, c2] = mx

    def scores_offdiag(i, buf):
        j, start = kv_start(i)
        before = j < jd
        qv = tuple(jnp.where(before, qb, qa) for qb, qa in zip(q_before, q_after))
        scores_into(buf, start, qv)

    def step(buf, c2, m, acc_ref, start):
        m_new = jnp.maximum(m, jnp.max(mx_ref[buf, c2], axis=-1, keepdims=True))
        pv = None
        for u in range(ksub):
            keys = slice(u * MXU_TILE, (u + 1) * MXU_TILE)
            p = jnp.exp2(s_ref[buf, c2, :, keys] - m_new).astype(BF16)
            part = _dot(p, va_ref[pl.ds(start + u * MXU_TILE, MXU_TILE), :])
            pv = part if pv is None else pv + part
        acc_ref[...] = jnp.exp2(m - m_new) * acc_ref[...] + pv
        return m_new

    def consume(buf, start, carry):
        return (step(buf, 0, carry[0], acc0_ref, start), step(buf, 1, carry[1], acc1_ref, start))

    neg = jnp.full((tq, 1), -jnp.inf, F32)
    diag_start = pl.multiple_of(jd * tk, tk)
    diag_bias = db_ref
    carry = (neg, neg)
    if not skip:
        scores_offdiag(jnp.int32(0), 0)
        for i in range(nkv):
            nxt = (i + 1) % SCORE_BUFFERS
            if i + 1 < nkv - 1:
                scores_offdiag(jnp.int32(i + 1), nxt)
            elif i + 1 == nkv - 1:
                scores_into(nxt, diag_start, q_diag, diag_bias)
            start = diag_start if i == nkv - 1 else kv_start(jnp.int32(i))[1]
            carry = consume(i % SCORE_BUFFERS, start, carry)
    else:
        def near_start(i, nb):
            j = jnp.where(i < nb, jd - 1 - i, jd + 1 + (i - nb))
            j = jnp.clip(j, 0, nkv - 1)
            return j, pl.multiple_of(j * tk, tk)

        def scores_near(i, nb, buf):
            j, start = near_start(i, nb)
            before = j < jd
            qv = tuple(jnp.where(before, qb, qa) for qb, qa in zip(q_before, q_after))
            scores_into(buf, start, qv)

        scores_into(0, diag_start, q_diag, diag_bias)
        scores_near(jnp.int32(0), jnp.minimum(jd, 1), 1)
        carry = consume(0, diag_start, carry)

        sq = jnp.square(q.astype(F32))
        q2 = (jnp.max(jnp.sum(jnp.where(low, sq, 0.0), axis=-1, keepdims=True), axis=0, keepdims=True),
              jnp.max(jnp.sum(jnp.where(low, 0.0, sq), axis=-1, keepdims=True), axis=0, keepdims=True))
        ub = jnp.maximum(jnp.sqrt(q2[0] * kn_ref[0][0:1, 0:1]), jnp.sqrt(q2[1] * kn_ref[1][0:1, 0:1]))
        ub = ub * 1.001 + 0.5
        m_min = jnp.min(jnp.minimum(carry[0], carry[1]), axis=0, keepdims=True)
        d = lax.broadcasted_iota(jnp.int32, (1, LANES), 1) + 1
        reach = ub - c * ((d - 1) * tk + 1).astype(F32) > m_min - SKIP_MARGIN
        needed = jnp.where(reach & (d <= nkv - 1), 1, 0)
        dist = jnp.sum(needed, axis=-1, keepdims=True)[0, 0]
        nb = jnp.minimum(dist, jd)
        n_off = nb + jnp.minimum(dist, nkv - 1 - jd)

        def pair(t, carry):
            i0 = 2 * t
            scores_near(i0 + 1, nb, 0)
            carry = consume(1, near_start(i0, nb)[1], carry)
            scores_near(i0 + 2, nb, 1)
            return consume(0, near_start(i0 + 1, nb)[1], carry)

        carry = lax.fori_loop(0, n_off // 2, pair, carry)

        @pl.when(n_off % 2 == 1)
        def _():
            consume(1, near_start(n_off - 1, nb)[1], carry)

    lam = (jnp.exp(jnp.sum(lq1_ref[...] * lk1_ref[...], axis=-1, keepdims=True))
           - jnp.exp(jnp.sum(lq2_ref[...] * lk2_ref[...], axis=-1, keepdims=True)) + lam_init)
    o = (acc0_ref[:, 0:LANES] / acc0_ref[:, LANES:2 * LANES]
         - lam * (acc1_ref[:, 0:LANES] / acc1_ref[:, LANES:2 * LANES]))
    ms = jnp.mean(o * o, axis=-1, keepdims=True)
    o_ref[...] = (o * lax.rsqrt(ms + LN_EPS) * g_ref[...] * (1.0 - lam_init)).astype(o_ref.dtype)


def _diffattn(hb, slopes, lq1, lk1, lq2, lk2, subln_g, *, bsz, seq, lam_init, tq, tk, head0, nheads, skip):
    n = bsz * seq
    nq = seq // tq
    const = lambda b, h, i, sl: (0, 0)
    grid_spec = pltpu.PrefetchScalarGridSpec(
        num_scalar_prefetch=1,
        grid=(bsz, nheads, nq),
        in_specs=[
            pl.BlockSpec((tq, LANES), lambda b, h, i, sl: (b * nq + i, head0 + h)),
            pl.BlockSpec((seq, LANES), lambda b, h, i, sl: (b, B_HEADS + head0 + h)),
            pl.BlockSpec((seq, LANES), lambda b, h, i, sl: (b, 2 * B_HEADS + head0 + h)),
            pl.BlockSpec((1, HEAD_DIM), const),
            pl.BlockSpec((1, HEAD_DIM), const),
            pl.BlockSpec((1, HEAD_DIM), const),
            pl.BlockSpec((1, HEAD_DIM), const),
            pl.BlockSpec((1, 2 * HEAD_DIM), const),
        ],
        out_specs=pl.BlockSpec((tq, LANES), lambda b, h, i, sl: (b * nq + i, h)),
        scratch_shapes=[pltpu.VMEM((seq, LANES), BF16), pltpu.VMEM((seq, LANES), BF16),
                        pltpu.VMEM((seq, 2 * LANES), BF16),
                        pltpu.VMEM((tq, 2 * LANES), F32), pltpu.VMEM((tq, 2 * LANES), F32),
                        pltpu.VMEM((SCORE_BUFFERS, 2, tq, tk), F32),
                        pltpu.VMEM((SCORE_BUFFERS, 2, tq, LANES), F32),
                        pltpu.VMEM((2, 8, LANES), F32),
                        pltpu.VMEM((tq, tk), F32)],
    )
    assert seq % tk == 0 and tq == tk and tk % MXU_TILE == 0 and seq // tk >= 2
    return pl.pallas_call(
        functools.partial(_diffattn_kernel, tq=tq, tk=tk, seq=seq, lam_init=lam_init, head0=head0, skip=skip),
        grid_spec=grid_spec,
        out_shape=jax.ShapeDtypeStruct((n, nheads * LANES), BF16),
        compiler_params=pltpu.CompilerParams(
            dimension_semantics=("arbitrary", "arbitrary", "arbitrary"),
            vmem_limit_bytes=VMEM_LIMIT_BYTES),
        name="diff_attention_near" if skip else "diff_attention",
    )(slopes, hb, hb, hb, lq1, lk1, lq2, lk2, subln_g)


def _na_tables(rows):
    wr = min(NA_ROWS_MAX, rows)
    nblk = rows // NA_Q_ROWS
    col = np.arange(GRID_W)
    col_start = np.clip(col - NA_COLS // 2, 0, GRID_W - NA_COLS)
    kc = np.arange(GRID_W)
    col_valid = (kc[None, :] >= col_start[:, None]) & (kc[None, :] < col_start[:, None] + NA_COLS)

    assert rows >= NA_WIN_ROWS and rows % NA_Q_ROWS == 0
    wstart, types, type_of = [], [], []
    for b in range(nblk):
        r = NA_Q_ROWS * b + np.arange(NA_Q_ROWS)
        r_start = np.clip(r - wr // 2, 0, rows - wr)
        w = int(min(r_start[0], rows - NA_WIN_ROWS))
        kr = w + np.arange(NA_WIN_ROWS)
        row_valid = (kr[None, :] >= r_start[:, None]) & (kr[None, :] < r_start[:, None] + wr)
        assert (row_valid.sum(axis=1) == wr).all()
        row_off0 = w - r + (NA_ROWS_MAX - 1)
        key = (row_off0.tobytes(), row_valid.tobytes())
        for t, (k0, _, _) in enumerate(types):
            if k0 == key:
                break
        else:
            t = len(types)
            types.append((key, row_off0, row_valid))
        wstart.append(w * GRID_W)
        type_of.append(t)
    return (np.asarray(wstart, np.int32), np.asarray(type_of, np.int32),
            [(t[1], t[2]) for t in types], col_valid)


def _na_bias_table(rpb, types, col_valid):
    nrow = rpb.shape[1]
    rp = jnp.pad(rpb.astype(F32) * LOG2E, ((0, 0), (0, 0), (GRID_W, GRID_W)))
    per_col = jnp.stack([rp[:, :, GRID_W + NA_COLS - 1 - c: 2 * GRID_W + NA_COLS - 1 - c]
                         for c in range(GRID_W)], axis=2)
    per_col = jnp.where(col_valid[None, None], per_col, NEG_BIG)
    per_col = jnp.pad(per_col, ((0, 0), (NA_WIN_ROWS, NA_WIN_ROWS), (0, 0), (0, 0)), constant_values=NEG_BIG)
    tabs = []
    for row_off0, row_valid in types:
        blocks = []
        for j in range(NA_Q_ROWS):
            lo = int(row_off0[j]) + NA_WIN_ROWS
            assert 0 <= lo and lo + NA_WIN_ROWS <= nrow + 2 * NA_WIN_ROWS
            blk = per_col[:, lo:lo + NA_WIN_ROWS]
            blk = jnp.where(row_valid[j][None, :, None, None], blk, NEG_BIG)
            blocks.append(jnp.transpose(blk, (0, 2, 1, 3)).reshape(C_HEADS, GRID_W, NA_WIN_ROWS * GRID_W))
        tabs.append(jnp.concatenate(blocks, axis=1))
    return jnp.stack(tabs)


def _natten_kernel(wstart_ref, type_ref, q_ref, k_ref, v_ref, tab_ref, o_ref, *, tq, win):
    del type_ref
    i = pl.program_id(1)
    start = pl.multiple_of(wstart_ref[i], GRID_W)
    lane = lax.broadcasted_iota(jnp.int32, (tq, LANES), 1)
    low = lane < HEAD_DIM
    for pair in range(C_HEADS // 2):
        cols = slice(pair * LANES, (pair + 1) * LANES)
        qp = q_ref[:, cols]
        kp = k_ref[pl.ds(start, win), cols]
        vp = jnp.concatenate([v_ref[pl.ds(start, win), cols], jnp.ones((win, LANES), BF16)], axis=1)
        zero = jnp.zeros_like(qp)
        outs = []
        for hh in range(2):
            qz = jnp.where(low if hh == 0 else jnp.logical_not(low), qp, zero)
            s = _dot_nt(qz, kp) + tab_ref[0, 2 * pair + hh]
            m = jnp.max(s, axis=-1, keepdims=True)
            p = jnp.exp2(s - m)
            pv = _dot(p.astype(BF16), vp)
            outs.append(pv[:, 0:LANES] / pv[:, LANES:2 * LANES])
        o_ref[:, cols] = jnp.where(low, outs[0], outs[1]).astype(o_ref.dtype)


def _natten(hc, tab, wstart, type_of, *, bsz, seq):
    n = bsz * seq
    tq = NA_Q_ROWS * GRID_W
    win = NA_WIN_ROWS * GRID_W
    nblk = seq // tq
    grid_spec = pltpu.PrefetchScalarGridSpec(
        num_scalar_prefetch=2,
        grid=(bsz, nblk),
        in_specs=[
            pl.BlockSpec((tq, C_WIDTH), lambda b, i, ws, ty: (b * nblk + i, 0)),
            pl.BlockSpec((seq, C_WIDTH), lambda b, i, ws, ty: (b, 1)),
            pl.BlockSpec((seq, C_WIDTH), lambda b, i, ws, ty: (b, 2)),
            pl.BlockSpec((1, C_HEADS, tq, win), lambda b, i, ws, ty: (ty[i], 0, 0, 0)),
        ],
        out_specs=pl.BlockSpec((tq, C_WIDTH), lambda b, i, ws, ty: (b * nblk + i, 0)),
    )
    return pl.pallas_call(
        functools.partial(_natten_kernel, tq=tq, win=win),
        grid_spec=grid_spec,
        out_shape=jax.ShapeDtypeStruct((n, C_WIDTH), BF16),
        compiler_params=pltpu.CompilerParams(
            dimension_semantics=("parallel", "parallel"), vmem_limit_bytes=VMEM_LIMIT_BYTES),
        name="neighbourhood_attention",
    )(wstart, type_of, hc, hc, hc, tab)


def _outffn_kernel(x_ref, oa_ref, ob0_ref, ob1_ref, oc_ref, wo_ref, wg_ref, wu_ref, wd_ref, lng_ref, lnb_ref,
                   y_ref):
    tm = x_ref.shape[0]
    halves = (slice(0, tm // 2), slice(tm // 2, tm))
    x1, x1b = [], []
    for rows in halves:
        mix = (_dot(oa_ref[rows, :], wo_ref[0:A_WIDTH, :])
               + _dot(ob0_ref[rows, :], wo_ref[A_WIDTH:A_WIDTH + SKIP_HEADS * LANES, :])
               + _dot(ob1_ref[rows, :], wo_ref[A_WIDTH + SKIP_HEADS * LANES:A_WIDTH + B_WIDTH, :])
               + _dot(oc_ref[rows, :], wo_ref[A_WIDTH + B_WIDTH:MIX_WIDTH, :]))
        x1.append(_layer_norm(DEEPNORM_ALPHA * x_ref[rows, :] + mix, lng_ref[0:1, :], lnb_ref[0:1, :]))
        x1b.append(x1[-1].astype(BF16))
    ff = [None, None]
    off = 0
    for width in FF_CHUNKS:
        cols = slice(off, off + width)
        for i in range(2):
            gate = _dot(x1b[i], wg_ref[:, cols])
            up = _dot(x1b[i], wu_ref[:, cols])
            part = _dot((jax.nn.silu(gate) * up).astype(BF16), wd_ref[cols, :])
            ff[i] = part if ff[i] is None else ff[i] + part
        off += width
    for i, rows in enumerate(halves):
        y_ref[rows, :] = _layer_norm(DEEPNORM_ALPHA * x1[i] + ff[i], lng_ref[1:2, :], lnb_ref[1:2, :])


def _outffn(x2d, oa, ob, oc, wo, wg, wu, wd, ln_g, ln_b, *, tm):
    n = x2d.shape[0]
    const = lambda i: (0, 0)
    row = lambda i: (i, 0)
    resident = functools.partial(pl.BlockSpec, index_map=const, pipeline_mode=pl.Buffered(1))
    return pl.pallas_call(
        _outffn_kernel,
        grid=(n // tm,),
        in_specs=[
            pl.BlockSpec((tm, D_MODEL), row),
            pl.BlockSpec((tm, A_WIDTH), row),
            pl.BlockSpec((tm, SKIP_HEADS * LANES), row),
            pl.BlockSpec((tm, B_WIDTH - SKIP_HEADS * LANES), row),
            pl.BlockSpec((tm, C_WIDTH), row),
            resident((MIX_WIDTH, D_MODEL)),
            resident((D_MODEL, D_FF)),
            resident((D_MODEL, D_FF)),
            resident((D_FF, D_MODEL)),
            pl.BlockSpec((2, D_MODEL), const),
            pl.BlockSpec((2, D_MODEL), const),
        ],
        out_specs=pl.BlockSpec((tm, D_MODEL), row),
        out_shape=jax.ShapeDtypeStruct((n, D_MODEL), F32),
        compiler_params=pltpu.CompilerParams(
            dimension_semantics=("parallel",), vmem_limit_bytes=VMEM_LIMIT_BYTES),
        name="outproj_ffn",
    )(x2d, oa, ob[0], ob[1], oc, wo, wg, wu, wd, ln_g, ln_b)


def _prepare_layer(l, w_in, w_out, a_ln_g, a_ln_b, a_w_s, a_b_s, c_rpb, w_gate, w_up, w_down):
    col_scale = np.ones((IN_WIDTH,), np.float32)
    col_scale[OFF_B:OFF_B + B_WIDTH] = QK_SCALE * LOG2E
    col_scale[OFF_C:OFF_C + C_WIDTH] = QK_SCALE * LOG2E
    return dict(
        w_in=(w_in[l] * col_scale).astype(BF16),
        w_out=w_out[l].astype(BF16),
        a_ln_g=a_ln_g[l][None, :],
        a_ln_b=a_ln_b[l][None, :],
        ws_all=a_w_s[l].reshape(A_GROUPS * CHUNK, CHUNK).astype(BF16),
        bs_tab=jnp.repeat(jnp.transpose(a_b_s[l]), HEAD_DIM, axis=1),
        rpb=c_rpb[l],
        w_gate=w_gate[l].astype(BF16),
        w_up=w_up[l].astype(BF16),
        w_down=w_down[l].astype(BF16),
    )


def _trunk(x, layers, lam_params, subln_g, ln_g, ln_b, slopes):
    bsz, seq, _ = x.shape
    rows = seq // GRID_W
    wstart, type_of, types, col_valid = _na_tables(rows)
    x2d = x.reshape(bsz * seq, D_MODEL)
    for l, p in enumerate(layers):
        lam_init = 0.8 - 0.6 * math.exp(-0.3 * l)
        oa, hb, hc = _inproj(x2d, p["w_in"], p["a_ln_g"], p["a_ln_b"], p["ws_all"], p["bs_tab"], tm=512)
        lq1, lk1, lq2, lk2 = (a[l][None, :] for a in lam_params)
        attn = functools.partial(_diffattn, hb, slopes, lq1, lk1, lq2, lk2, subln_g[l][None, :],
                                 bsz=bsz, seq=seq, lam_init=lam_init, tq=512, tk=512)
        ob = (attn(head0=0, nheads=SKIP_HEADS, skip=True),
              attn(head0=SKIP_HEADS, nheads=B_HEADS - SKIP_HEADS, skip=False))
        tab = _na_bias_table(p["rpb"], types, col_valid)
        oc = _natten(hc, tab, jnp.asarray(wstart), jnp.asarray(type_of), bsz=bsz, seq=seq)
        x2d = _outffn(x2d, oa, ob, oc, p["w_out"], p["w_gate"], p["w_up"], p["w_down"],
                      ln_g[l], ln_b[l], tm=1024)
    return x2d.reshape(bsz, seq, D_MODEL)


def kernel(x_prompt, x_sample, w_in, w_out, a_ln_g, a_ln_b, a_w_s, a_b_s, b_lambda_q1, b_lambda_k1,
           b_lambda_q2, b_lambda_k2, b_subln_g, c_rpb, w_gate, w_up, w_down, ln_g, ln_b):
    layers = [_prepare_layer(l, w_in, w_out, a_ln_g, a_ln_b, a_w_s, a_b_s, c_rpb, w_gate, w_up, w_down)
              for l in range(DEPTH)]
    lam_params = (b_lambda_q1, b_lambda_k1, b_lambda_q2, b_lambda_k2)
    slopes = jnp.exp2(-8.0 * jnp.arange(1, B_HEADS + 1, dtype=F32) / B_HEADS)
    y_prompt = _trunk(x_prompt, layers, lam_params, b_subln_g, ln_g, ln_b, slopes)
    y_sample = _trunk(x_sample, layers, lam_params, b_subln_g, ln_g, ln_b, slopes)
    return (y_prompt, y_sample)
```

```python
import functools
import math

import jax
import jax.numpy as jnp
import numpy as np
from jax import lax
from jax.experimental import pallas as pl
from jax.experimental.pallas import tpu as pltpu

F32 = jnp.float32
BF16 = jnp.bfloat16

D_MODEL = 1024
DEPTH = 2
HEAD_DIM = 64
A_GROUPS = 4
A_WIDTH = A_GROUPS * HEAD_DIM
CHUNK = 128
B_HEADS = 4
B_WIDTH = B_HEADS * 2 * HEAD_DIM
C_HEADS = 4
C_WIDTH = C_HEADS * HEAD_DIM
MIX_WIDTH = A_WIDTH + B_WIDTH + C_WIDTH
IN_WIDTH = 2 * A_WIDTH + 3 * B_WIDTH + 3 * C_WIDTH
GRID_W = 64
NA_ROWS_MAX = 8
NA_COLS = 16
D_FF = 2816
DEEPNORM_ALPHA = (2 * DEPTH) ** 0.25
LN_EPS = 1e-5
QK_SCALE = HEAD_DIM ** -0.5
LOG2E = math.log2(math.e)

LANES = 128
MXU_TILE = 256
SCORE_BUFFERS = 2
VMEM_LIMIT_BYTES = 56 * 1024 * 1024

OFF_AU, OFF_AV = 0, A_WIDTH
OFF_B = 2 * A_WIDTH
OFF_C = OFF_B + 3 * B_WIDTH

NA_Q_ROWS = 4
NA_WIN_ROWS = 12
NEG_BIG = -1e30

SKIP_HEADS = 1
SKIP_MARGIN = 160.0

FF_CHUNKS = (768, 768, 768, 512)
assert sum(FF_CHUNKS) == D_FF


def _layer_norm(x, g, b):
    mu = jnp.mean(x, axis=-1, keepdims=True)
    xc = x - mu
    var = jnp.mean(xc * xc, axis=-1, keepdims=True)
    return xc * lax.rsqrt(var + LN_EPS) * g + b


def _dot(a, b):
    return jnp.dot(a, b, preferred_element_type=F32)


def _dot_nt(a, b):
    return lax.dot_general(a, b, (((1,), (1,)), ((), ())), preferred_element_type=F32)


def _inproj_kernel(x_ref, w_ref, lng_ref, lnb_ref, ws_ref, bs_ref, oa_ref, hb_ref, hc_ref, *, tm):
    xb = x_ref[...].astype(BF16)
    ha = jax.nn.gelu(_dot(xb, w_ref[:, 0:OFF_B]))
    hb_ref[...] = _dot(xb, w_ref[:, OFF_B:OFF_C]).astype(BF16)
    hc_ref[...] = _dot(xb, w_ref[:, OFF_C:IN_WIDTH]).astype(BF16)

    u = ha[:, :A_WIDTH]
    v = _layer_norm(ha[:, A_WIDTH:], lng_ref[...], lnb_ref[...]).astype(BF16)
    lane = lax.broadcasted_iota(jnp.int32, (CHUNK, A_WIDTH), 1)
    ws = ws_ref[...]
    for n in range(tm // CHUNK):
        rows = slice(n * CHUNK, (n + 1) * CHUNK)
        res = _dot(ws, v[rows, :])
        mixed = res[0:CHUNK]
        for g in range(1, A_GROUPS):
            mixed = jnp.where(lane >= g * HEAD_DIM, res[g * CHUNK:(g + 1) * CHUNK], mixed)
        oa_ref[rows, :] = (u[rows, :] * (mixed + bs_ref[...])).astype(BF16)


def _inproj(x2d, w_in_b, a_ln_g, a_ln_b, ws_all, bs_tab, *, tm):
    n = x2d.shape[0]
    const = lambda i: (0, 0)
    return pl.pallas_call(
        functools.partial(_inproj_kernel, tm=tm),
        grid=(n // tm,),
        in_specs=[
            pl.BlockSpec((tm, D_MODEL), lambda i: (i, 0)),
            pl.BlockSpec((D_MODEL, IN_WIDTH), const),
            pl.BlockSpec((1, A_WIDTH), const),
            pl.BlockSpec((1, A_WIDTH), const),
            pl.BlockSpec((A_GROUPS * CHUNK, CHUNK), const),
            pl.BlockSpec((CHUNK, A_WIDTH), const),
        ],
        out_specs=[
            pl.BlockSpec((tm, A_WIDTH), lambda i: (i, 0)),
            pl.BlockSpec((tm, 3 * B_WIDTH), lambda i: (i, 0)),
            pl.BlockSpec((tm, 3 * C_WIDTH), lambda i: (i, 0)),
        ],
        out_shape=[
            jax.ShapeDtypeStruct((n, A_WIDTH), BF16),
            jax.ShapeDtypeStruct((n, 3 * B_WIDTH), BF16),
            jax.ShapeDtypeStruct((n, 3 * C_WIDTH), BF16),
        ],
        compiler_params=pltpu.CompilerParams(
            dimension_semantics=("parallel",), vmem_limit_bytes=VMEM_LIMIT_BYTES),
        name="inproj_gating",
    )(x2d, w_in_b, a_ln_g, a_ln_b, ws_all, bs_tab)


def _split3(x):
    hi = x.astype(BF16).astype(F32)
    r = x - hi
    mid = r.astype(BF16).astype(F32)
    return hi, mid, r - mid


def _aug_lanes(x, lane, value_first, one):
    l6 = jnp.bitwise_and(lane, HEAD_DIM - 1)
    hi, mid, lo = _split3(x)
    v0 = 0 if value_first else 3
    out = jnp.where(l6 == v0, hi, jnp.where(l6 == v0 + 1, mid, jnp.where(l6 == v0 + 2, lo, 0.0)))
    ones = (l6 >= 3 - v0) & (l6 < 6 - v0)
    return jnp.where(ones, one, out)


def _diffattn_kernel(slopes_ref, q_ref, k_ref, v_ref, lq1_ref, lk1_ref, lq2_ref, lk2_ref, g_ref,
                     o_ref, ka0_ref, ka1_ref, va_ref, acc0_ref, acc1_ref, s_ref, mx_ref, kn_ref, db_ref,
                     *, tq, tk, seq, lam_init, head0, skip):
    h = pl.program_id(1) + head0
    qi = pl.program_id(2)
    c = slopes_ref[h] * LOG2E
    fill = 512
    lane_f = lax.broadcasted_iota(jnp.int32, (fill, LANES), 1)
    row_f = lax.broadcasted_iota(jnp.int32, (fill, LANES), 0)

    @pl.when(qi == 0)
    def _():
        def fill_body(r, carry):
            start = pl.multiple_of(r * fill, fill)
            pos = (row_f + r * fill).astype(F32)
            aug = _aug_lanes(c * pos, lane_f, False, 1.0).astype(BF16)
            kk = k_ref[pl.ds(start, fill), :]
            ka0_ref[pl.ds(start, fill), :] = jnp.where(lane_f < HEAD_DIM, kk, aug)
            ka1_ref[pl.ds(start, fill), :] = jnp.where(lane_f < HEAD_DIM, aug, kk)
            va_ref[pl.ds(start, fill), 0:LANES] = v_ref[pl.ds(start, fill), :]
            va_ref[pl.ds(start, fill), LANES:2 * LANES] = jnp.ones((fill, LANES), BF16)
            if not skip:
                return carry
            sq = jnp.square(kk.astype(F32))
            n0 = jnp.sum(jnp.where(lane_f < HEAD_DIM, sq, 0.0), axis=-1, keepdims=True)
            n1 = jnp.sum(jnp.where(lane_f < HEAD_DIM, 0.0, sq), axis=-1, keepdims=True)
            return jnp.maximum(carry[0], n0), jnp.maximum(carry[1], n1)
        init = (jnp.zeros((fill, 1), F32),) * 2 if skip else 0
        norms = lax.fori_loop(0, seq // fill, fill_body, init)
        rel = (lax.broadcasted_iota(jnp.int32, (tq, tk), 0)
               - lax.broadcasted_iota(jnp.int32, (tq, tk), 1))
        db_ref[...] = c * jnp.abs(rel).astype(F32)
        if skip:
            for c2 in range(2):
                kn_ref[c2] = jnp.broadcast_to(jnp.max(norms[c2], axis=0, keepdims=True), (8, LANES))

    q = q_ref[...]
    lane = lax.broadcasted_iota(jnp.int32, (tq, LANES), 1)
    low = lane < HEAD_DIM
    t = (lax.broadcasted_iota(jnp.int32, (tq, LANES), 0) + qi * tq).astype(F32)
    aug_before = _aug_lanes(-(c * t), lane, True, 1.0)
    zero = jnp.zeros_like(q)

    def variants(aug):
        return jnp.where(low, q, aug), jnp.where(low, aug, q)

    q_before = variants(aug_before.astype(BF16))
    q_after = variants((-aug_before).astype(BF16))
    q_diag = variants(zero)

    acc0_ref[...] = jnp.zeros_like(acc0_ref)
    acc1_ref[...] = jnp.zeros_like(acc1_ref)

    nkv = seq // tk
    jd = (qi * tq) // tk
    ksub = tk // MXU_TILE

    def kv_start(i):
        j = i + (i >= jd).astype(jnp.int32)
        return j, pl.multiple_of(j * tk, tk)

    def scores_into(buf, start, qv, bias=None):
        for c2, ka_ref in enumerate((ka0_ref, ka1_ref)):
            mx = None
            for u in range(ksub):
                keys = slice(u * MXU_TILE, (u + 1) * MXU_TILE)
                s = _dot_nt(qv[c2], ka_ref[pl.ds(start + u * MXU_TILE, MXU_TILE), :])
                if bias is not None:
                    s = s - bias[:, keys]
                s_ref[buf, c2, :, keys] = s
                for n in range(MXU_TILE // LANES):
                    tile = s[:, n * LANES:(n + 1) * LANES]
                    mx = tile if mx is None else jnp.maximum(mx, tile)
            mx_ref[buf, c2] = mx

    def scores_offdiag(i, buf):
        j, start = kv_start(i)
        before = j < jd
        qv = tuple(jnp.where(before, qb, qa) for qb, qa in zip(q_before, q_after))
        scores_into(buf, start, qv)

    def step(buf, c2, m, acc_ref, start):
        m_new = jnp.maximum(m, jnp.max(mx_ref[buf, c2], axis=-1, keepdims=True))
        pv = None
        for u in range(ksub):
            keys = slice(u * MXU_TILE, (u + 1) * MXU_TILE)
            p = jnp.exp2((s_ref[buf, c2, :, keys] - m_new).astype(BF16))
            part = _dot(p, va_ref[pl.ds(start + u * MXU_TILE, MXU_TILE), :])
            pv = part if pv is None else pv + part
        acc_ref[...] = jnp.exp2(m - m_new) * acc_ref[...] + pv
        return m_new

    def consume(buf, start, carry):
        return (step(buf, 0, carry[0], acc0_ref, start), step(buf, 1, carry[1], acc1_ref, start))

    neg = jnp.full((tq, 1), -jnp.inf, F32)
    diag_start = pl.multiple_of(jd * tk, tk)
    diag_bias = db_ref
    carry = (neg, neg)
    if not skip:
        scores_offdiag(jnp.int32(0), 0)
        for i in range(nkv):
            nxt = (i + 1) % SCORE_BUFFERS
            if i + 1 < nkv - 1:
                scores_offdiag(jnp.int32(i + 1), nxt)
            elif i + 1 == nkv - 1:
                scores_into(nxt, diag_start, q_diag, diag_bias)
            start = diag_start if i == nkv - 1 else kv_start(jnp.int32(i))[1]
            carry = consume(i % SCORE_BUFFERS, start, carry)
    else:
        def near_start(i, nb):
            j = jnp.where(i < nb, jd - 1 - i, jd + 1 + (i - nb))
            j = jnp.clip(j, 0, nkv - 1)
            return j, pl.multiple_of(j * tk, tk)

        def scores_near(i, nb, buf):
            j, start = near_start(i, nb)
            before = j < jd
            qv = tuple(jnp.where(before, qb, qa) for qb, qa in zip(q_before, q_after))
            scores_into(buf, start, qv)

        scores_into(0, diag_start, q_diag, diag_bias)
        scores_near(jnp.int32(0), jnp.minimum(jd, 1), 1)
        carry = consume(0, diag_start, carry)

        sq = jnp.square(q.astype(F32))
        q2 = (jnp.max(jnp.sum(jnp.where(low, sq, 0.0), axis=-1, keepdims=True), axis=0, keepdims=True),
              jnp.max(jnp.sum(jnp.where(low, 0.0, sq), axis=-1, keepdims=True), axis=0, keepdims=True))
        ub = jnp.maximum(jnp.sqrt(q2[0] * kn_ref[0][0:1, 0:1]), jnp.sqrt(q2[1] * kn_ref[1][0:1, 0:1]))
        ub = ub * 1.001 + 0.5
        m_min = jnp.min(jnp.minimum(carry[0], carry[1]), axis=0, keepdims=True)
        d = lax.broadcasted_iota(jnp.int32, (1, LANES), 1) + 1
        reach = ub - c * ((d - 1) * tk + 1).astype(F32) > m_min - SKIP_MARGIN
        needed = jnp.where(reach & (d <= nkv - 1), 1, 0)
        dist = jnp.sum(needed, axis=-1, keepdims=True)[0, 0]
        nb = jnp.minimum(dist, jd)
        n_off = nb + jnp.minimum(dist, nkv - 1 - jd)

        def pair(t, carry):
            i0 = 2 * t
            scores_near(i0 + 1, nb, 0)
            carry = consume(1, near_start(i0, nb)[1], carry)
            scores_near(i0 + 2, nb, 1)
            return consume(0, near_start(i0 + 1, nb)[1], carry)

        carry = lax.fori_loop(0, n_off // 2, pair, carry)

        @pl.when(n_off % 2 == 1)
        def _():
            consume(1, near_start(n_off - 1, nb)[1], carry)

    lam = (jnp.exp(jnp.sum(lq1_ref[...] * lk1_ref[...], axis=-1, keepdims=True))
           - jnp.exp(jnp.sum(lq2_ref[...] * lk2_ref[...], axis=-1, keepdims=True)) + lam_init)
    o = (acc0_ref[:, 0:LANES] / acc0_ref[:, LANES:2 * LANES]
         - lam * (acc1_ref[:, 0:LANES] / acc1_ref[:, LANES:2 * LANES]))
    ms = jnp.mean(o * o, axis=-1, keepdims=True)
    o_ref[...] = (o * lax.rsqrt(ms + LN_EPS) * g_ref[...] * (1.0 - lam_init)).astype(o_ref.dtype)


def _diffattn(hb, slopes, lq1, lk1, lq2, lk2, subln_g, *, bsz, seq, lam_init, tq, tk, head0, nheads, skip):
    n = bsz * seq
    nq = seq // tq
    const = lambda b, h, i, sl: (0, 0)
    grid_spec = pltpu.PrefetchScalarGridSpec(
        num_scalar_prefetch=1,
        grid=(bsz, nheads, nq),
        in_specs=[
            pl.BlockSpec((tq, LANES), lambda b, h, i, sl: (b * nq + i, head0 + h)),
            pl.BlockSpec((seq, LANES), lambda b, h, i, sl: (b, B_HEADS + head0 + h)),
            pl.BlockSpec((seq, LANES), lambda b, h, i, sl: (b, 2 * B_HEADS + head0 + h)),
            pl.BlockSpec((1, HEAD_DIM), const),
            pl.BlockSpec((1, HEAD_DIM), const),
            pl.BlockSpec((1, HEAD_DIM), const),
            pl.BlockSpec((1, HEAD_DIM), const),
            pl.BlockSpec((1, 2 * HEAD_DIM), const),
        ],
        out_specs=pl.BlockSpec((tq, LANES), lambda b, h, i, sl: (b * nq + i, h)),
        scratch_shapes=[pltpu.VMEM((seq, LANES), BF16), pltpu.VMEM((seq, LANES), BF16),
                        pltpu.VMEM((seq, 2 * LANES), BF16),
                        pltpu.VMEM((tq, 2 * LANES), F32), pltpu.VMEM((tq, 2 * LANES), F32),
                        pltpu.VMEM((SCORE_BUFFERS, 2, tq, tk), F32),
                        pltpu.VMEM((SCORE_BUFFERS, 2, tq, LANES), F32),
                        pltpu.VMEM((2, 8, LANES), F32),
                        pltpu.VMEM((tq, tk), F32)],
    )
    assert seq % tk == 0 and tq == tk and tk % MXU_TILE == 0 and seq // tk >= 2
    return pl.pallas_call(
        functools.partial(_diffattn_kernel, tq=tq, tk=tk, seq=seq, lam_init=lam_init, head0=head0, skip=skip),
        grid_spec=grid_spec,
        out_shape=jax.ShapeDtypeStruct((n, nheads * LANES), BF16),
        compiler_params=pltpu.CompilerParams(
            dimension_semantics=("arbitrary", "arbitrary", "arbitrary"),
            vmem_limit_bytes=VMEM_LIMIT_BYTES),
        name="diff_attention_near" if skip else "diff_attention",
    )(slopes, hb, hb, hb, lq1, lk1, lq2, lk2, subln_g)


def _na_tables(rows):
    wr = min(NA_ROWS_MAX, rows)
    nblk = rows // NA_Q_ROWS
    col = np.arange(GRID_W)
    col_start = np.clip(col - NA_COLS // 2, 0, GRID_W - NA_COLS)
    kc = np.arange(GRID_W)
    col_valid = (kc[None, :] >= col_start[:, None]) & (kc[None, :] < col_start[:, None] + NA_COLS)

    assert rows >= NA_WIN_ROWS and rows % NA_Q_ROWS == 0
    wstart, types, type_of = [], [], []
    for b in range(nblk):
        r = NA_Q_ROWS * b + np.arange(NA_Q_ROWS)
        r_start = np.clip(r - wr // 2, 0, rows - wr)
        w = int(min(r_start[0], rows - NA_WIN_ROWS))
        kr = w + np.arange(NA_WIN_ROWS)
        row_valid = (kr[None, :] >= r_start[:, None]) & (kr[None, :] < r_start[:, None] + wr)
        assert (row_valid.sum(axis=1) == wr).all()
        row_off0 = w - r + (NA_ROWS_MAX - 1)
        key = (row_off0.tobytes(), row_valid.tobytes())
        for t, (k0, _, _) in enumerate(types):
            if k0 == key:
                break
        else:
            t = len(types)
            types.append((key, row_off0, row_valid))
        wstart.append(w * GRID_W)
        type_of.append(t)
    return (np.asarray(wstart, np.int32), np.asarray(type_of, np.int32),
            [(t[1], t[2]) for t in types], col_valid)


def _na_bias_table(rpb, types, col_valid):
    nrow = rpb.shape[1]
    rp = jnp.pad(rpb.astype(F32) * LOG2E, ((0, 0), (0, 0), (GRID_W, GRID_W)))
    per_col = jnp.stack([rp[:, :, GRID_W + NA_COLS - 1 - c: 2 * GRID_W + NA_COLS - 1 - c]
                         for c in range(GRID_W)], axis=2)
    per_col = jnp.where(col_valid[None, None], per_col, NEG_BIG)
    per_col = jnp.pad(per_col, ((0, 0), (NA_WIN_ROWS, NA_WIN_ROWS), (0, 0), (0, 0)), constant_values=NEG_BIG)
    tabs = []
    for row_off0, row_valid in types:
        blocks = []
        for j in range(NA_Q_ROWS):
            lo = int(row_off0[j]) + NA_WIN_ROWS
            assert 0 <= lo and lo + NA_WIN_ROWS <= nrow + 2 * NA_WIN_ROWS
            blk = per_col[:, lo:lo + NA_WIN_ROWS]
            blk = jnp.where(row_valid[j][None, :, None, None], blk, NEG_BIG)
            blocks.append(jnp.transpose(blk, (0, 2, 1, 3)).reshape(C_HEADS, GRID_W, NA_WIN_ROWS * GRID_W))
        tabs.append(jnp.concatenate(blocks, axis=1))
    return jnp.stack(tabs)


def _natten_kernel(wstart_ref, type_ref, q_ref, k_ref, v_ref, tab_ref, o_ref, *, tq, win):
    del type_ref
    i = pl.program_id(1)
    start = pl.multiple_of(wstart_ref[i], GRID_W)
    lane = lax.broadcasted_iota(jnp.int32, (tq, LANES), 1)
    low = lane < HEAD_DIM
    for pair in range(C_HEADS // 2):
        cols = slice(pair * LANES, (pair + 1) * LANES)
        qp = q_ref[:, cols]
        kp = k_ref[pl.ds(start, win), cols]
        vp = jnp.concatenate([v_ref[pl.ds(start, win), cols], jnp.ones((win, LANES), BF16)], axis=1)
        zero = jnp.zeros_like(qp)
        outs = []
        for hh in range(2):
            qz = jnp.where(low if hh == 0 else jnp.logical_not(low), qp, zero)
            s = _dot_nt(qz, kp) + tab_ref[0, 2 * pair + hh]
            m = jnp.max(s, axis=-1, keepdims=True)
            p = jnp.exp2((s - m).astype(BF16))
            pv = _dot(p, vp)
            outs.append(pv[:, 0:LANES] / pv[:, LANES:2 * LANES])
        o_ref[:, cols] = jnp.where(low, outs[0], outs[1]).astype(o_ref.dtype)


def _natten(hc, tab, wstart, type_of, *, bsz, seq):
    n = bsz * seq
    tq = NA_Q_ROWS * GRID_W
    win = NA_WIN_ROWS * GRID_W
    nblk = seq // tq
    grid_spec = pltpu.PrefetchScalarGridSpec(
        num_scalar_prefetch=2,
        grid=(bsz, nblk),
        in_specs=[
            pl.BlockSpec((tq, C_WIDTH), lambda b, i, ws, ty: (b * nblk + i, 0)),
            pl.BlockSpec((seq, C_WIDTH), lambda b, i, ws, ty: (b, 1)),
            pl.BlockSpec((seq, C_WIDTH), lambda b, i, ws, ty: (b, 2)),
            pl.BlockSpec((1, C_HEADS, tq, win), lambda b, i, ws, ty: (ty[i], 0, 0, 0)),
        ],
        out_specs=pl.BlockSpec((tq, C_WIDTH), lambda b, i, ws, ty: (b * nblk + i, 0)),
    )
    return pl.pallas_call(
        functools.partial(_natten_kernel, tq=tq, win=win),
        grid_spec=grid_spec,
        out_shape=jax.ShapeDtypeStruct((n, C_WIDTH), BF16),
        compiler_params=pltpu.CompilerParams(
            dimension_semantics=("parallel", "parallel"), vmem_limit_bytes=VMEM_LIMIT_BYTES),
        name="neighbourhood_attention",
    )(wstart, type_of, hc, hc, hc, tab)


def _outffn_kernel(x_ref, oa_ref, ob0_ref, ob1_ref, oc_ref, wo_ref, wg_ref, wu_ref, wd_ref, lng_ref, lnb_ref,
                   y_ref):
    tm = x_ref.shape[0]
    halves = (slice(0, tm // 2), slice(tm // 2, tm))
    x1, x1b = [], []
    for rows in halves:
        mix = (_dot(oa_ref[rows, :], wo_ref[0:A_WIDTH, :])
               + _dot(jnp.concatenate([ob0_ref[rows, :], ob1_ref[rows, :]], axis=1),
                      wo_ref[A_WIDTH:A_WIDTH + B_WIDTH, :])
               + _dot(oc_ref[rows, :], wo_ref[A_WIDTH + B_WIDTH:MIX_WIDTH, :]))
        x1.append(_layer_norm(DEEPNORM_ALPHA * x_ref[rows, :] + mix, lng_ref[0:1, :], lnb_ref[0:1, :]))
        x1b.append(x1[-1].astype(BF16))
    ff = [None, None]
    off = 0
    for width in FF_CHUNKS:
        cols = slice(off, off + width)
        for i in range(2):
            gate = _dot(x1b[i], wg_ref[:, cols])
            up = _dot(x1b[i], wu_ref[:, cols])
            part = _dot((jax.nn.silu(gate) * up).astype(BF16), wd_ref[cols, :])
            ff[i] = part if ff[i] is None else ff[i] + part
        off += width
    for i, rows in enumerate(halves):
        y_ref[rows, :] = _layer_norm(DEEPNORM_ALPHA * x1[i] + ff[i], lng_ref[1:2, :], lnb_ref[1:2, :])


def _outffn(x2d, oa, ob, oc, wo, wg, wu, wd, ln_g, ln_b, *, tm):
    n = x2d.shape[0]
    const = lambda i: (0, 0)
    row = lambda i: (i, 0)
    resident = functools.partial(pl.BlockSpec, index_map=const, pipeline_mode=pl.Buffered(1))
    return pl.pallas_call(
        _outffn_kernel,
        grid=(n // tm,),
        in_specs=[
            pl.BlockSpec((tm, D_MODEL), row),
            pl.BlockSpec((tm, A_WIDTH), row),
            pl.BlockSpec((tm, SKIP_HEADS * LANES), row),
            pl.BlockSpec((tm, B_WIDTH - SKIP_HEADS * LANES), row),
            pl.BlockSpec((tm, C_WIDTH), row),
            resident((MIX_WIDTH, D_MODEL)),
            resident((D_MODEL, D_FF)),
            resident((D_MODEL, D_FF)),
            resident((D_FF, D_MODEL)),
            pl.BlockSpec((2, D_MODEL), const),
            pl.BlockSpec((2, D_MODEL), const),
        ],
        out_specs=pl.BlockSpec((tm, D_MODEL), row),
        out_shape=jax.ShapeDtypeStruct((n, D_MODEL), F32),
        compiler_params=pltpu.CompilerParams(
            dimension_semantics=("parallel",), vmem_limit_bytes=VMEM_LIMIT_BYTES),
        name="outproj_ffn",
    )(x2d, oa, ob[0], ob[1], oc, wo, wg, wu, wd, ln_g, ln_b)


def _prepare_layer(l, w_in, w_out, a_ln_g, a_ln_b, a_w_s, a_b_s, c_rpb, w_gate, w_up, w_down):
    col_scale = np.ones((IN_WIDTH,), np.float32)
    col_scale[OFF_B:OFF_B + B_WIDTH] = QK_SCALE * LOG2E
    col_scale[OFF_C:OFF_C + C_WIDTH] = QK_SCALE * LOG2E
    return dict(
        w_in=(w_in[l] * col_scale).astype(BF16),
        w_out=w_out[l].astype(BF16),
        a_ln_g=a_ln_g[l][None, :],
        a_ln_b=a_ln_b[l][None, :],
        ws_all=a_w_s[l].reshape(A_GROUPS * CHUNK, CHUNK).astype(BF16),
        bs_tab=jnp.repeat(jnp.transpose(a_b_s[l]), HEAD_DIM, axis=1),
        rpb=c_rpb[l],
        w_gate=w_gate[l].astype(BF16),
        w_up=w_up[l].astype(BF16),
        w_down=w_down[l].astype(BF16),
    )


def _trunk(x, layers, lam_params, subln_g, ln_g, ln_b, slopes):
    bsz, seq, _ = x.shape
    rows = seq // GRID_W
    wstart, type_of, types, col_valid = _na_tables(rows)
    x2d = x.reshape(bsz * seq, D_MODEL)
    for l, p in enumerate(layers):
        lam_init = 0.8 - 0.6 * math.exp(-0.3 * l)
        oa, hb, hc = _inproj(x2d, p["w_in"], p["a_ln_g"], p["a_ln_b"], p["ws_all"], p["bs_tab"], tm=512)
        lq1, lk1, lq2, lk2 = (a[l][None, :] for a in lam_params)
        attn = functools.partial(_diffattn, hb, slopes, lq1, lk1, lq2, lk2, subln_g[l][None, :],
                                 bsz=bsz, seq=seq, lam_init=lam_init, tq=512, tk=512)
        ob = (attn(head0=0, nheads=SKIP_HEADS, skip=True),
              attn(head0=SKIP_HEADS, nheads=B_HEADS - SKIP_HEADS, skip=False))
        tab = _na_bias_table(p["rpb"], types, col_valid)
        oc = _natten(hc, tab, jnp.asarray(wstart), jnp.asarray(type_of), bsz=bsz, seq=seq)
        x2d = _outffn(x2d, oa, ob, oc, p["w_out"], p["w_gate"], p["w_up"], p["w_down"],
                      ln_g[l], ln_b[l], tm=1024)
    return x2d.reshape(bsz, seq, D_MODEL)


def kernel(x_prompt, x_sample, w_in, w_out, a_ln_g, a_ln_b, a_w_s, a_b_s, b_lambda_q1, b_lambda_k1,
           b_lambda_q2, b_lambda_k2, b_subln_g, c_rpb, w_gate, w_up, w_down, ln_g, ln_b):
    layers = [_prepare_layer(l, w_in, w_out, a_ln_g, a_ln_b, a_w_s, a_b_s, c_rpb, w_gate, w_up, w_down)
              for l in range(DEPTH)]
    lam_params = (b_lambda_q1, b_lambda_k1, b_lambda_q2, b_lambda_k2)
    slopes = jnp.exp2(-8.0 * jnp.arange(1, B_HEADS + 1, dtype=F32) / B_HEADS)
    y_prompt = _trunk(x_prompt, layers, lam_params, b_subln_g, ln_g, ln_b, slopes)
    y_sample = _trunk(x_sample, layers, lam_params, b_subln_g, ln_g, ln_b, slopes)
    return (y_prompt, y_sample)
```

```python
import functools
import math

import jax
import jax.numpy as jnp
import numpy as np
from jax import lax
from jax.experimental import pallas as pl
from jax.experimental.pallas import tpu as pltpu

F32 = jnp.float32
BF16 = jnp.bfloat16

D_MODEL = 1024
DEPTH = 2
HEAD_DIM = 64
A_GROUPS = 4
A_WIDTH = A_GROUPS * HEAD_DIM
CHUNK = 128
B_HEADS = 4
B_WIDTH = B_HEADS * 2 * HEAD_DIM
C_HEADS = 4
C_WIDTH = C_HEADS * HEAD_DIM
MIX_WIDTH = A_WIDTH + B_WIDTH + C_WIDTH
IN_WIDTH = 2 * A_WIDTH + 3 * B_WIDTH + 3 * C_WIDTH
GRID_W = 64
NA_ROWS_MAX = 8
NA_COLS = 16
D_FF = 2816
DEEPNORM_ALPHA = (2 * DEPTH) ** 0.25
LN_EPS = 1e-5
QK_SCALE = HEAD_DIM ** -0.5
LOG2E = math.log2(math.e)

LANES = 128
MXU_TILE = 256
SCORE_BUFFERS = 2
VMEM_LIMIT_BYTES = 56 * 1024 * 1024

OFF_AU, OFF_AV = 0, A_WIDTH
OFF_B = 2 * A_WIDTH
OFF_C = OFF_B + 3 * B_WIDTH

NA_Q_ROWS = 4
NA_WIN_ROWS = 12
NEG_BIG = -1e30

SKIP_HEADS = 1
SKIP_MARGIN = 160.0

FF_CHUNKS = (768, 768, 768, 512)
assert sum(FF_CHUNKS) == D_FF


def _layer_norm(x, g, b):
    mu = jnp.mean(x, axis=-1, keepdims=True)
    xc = x - mu
    var = jnp.mean(xc * xc, axis=-1, keepdims=True)
    return xc * lax.rsqrt(var + LN_EPS) * g + b


def _dot(a, b):
    return jnp.dot(a, b, preferred_element_type=F32)


def _dot_nt(a, b):
    return lax.dot_general(a, b, (((1,), (1,)), ((), ())), preferred_element_type=F32)


def _inproj_kernel(x_ref, w_ref, lng_ref, lnb_ref, ws_ref, bs_ref, oa_ref, hb_ref, hc_ref, *, tm):
    xb = x_ref[...].astype(BF16)
    ha = jax.nn.gelu(_dot(xb, w_ref[:, 0:OFF_B]))
    hb_ref[...] = _dot(xb, w_ref[:, OFF_B:OFF_C]).astype(BF16)
    hc_ref[...] = _dot(xb, w_ref[:, OFF_C:IN_WIDTH]).astype(BF16)

    u = ha[:, :A_WIDTH]
    v = _layer_norm(ha[:, A_WIDTH:], lng_ref[...], lnb_ref[...]).astype(BF16)
    lane = lax.broadcasted_iota(jnp.int32, (CHUNK, A_WIDTH), 1)
    ws = ws_ref[...]
    for n in range(tm // CHUNK):
        rows = slice(n * CHUNK, (n + 1) * CHUNK)
        res = _dot(ws, v[rows, :])
        mixed = res[0:CHUNK]
        for g in range(1, A_GROUPS):
            mixed = jnp.where(lane >= g * HEAD_DIM, res[g * CHUNK:(g + 1) * CHUNK], mixed)
        oa_ref[rows, :] = (u[rows, :] * (mixed + bs_ref[...])).astype(BF16)


def _inproj(x2d, w_in_b, a_ln_g, a_ln_b, ws_all, bs_tab, *, tm):
    n = x2d.shape[0]
    const = lambda i: (0, 0)
    return pl.pallas_call(
        functools.partial(_inproj_kernel, tm=tm),
        grid=(n // tm,),
        in_specs=[
            pl.BlockSpec((tm, D_MODEL), lambda i: (i, 0)),
            pl.BlockSpec((D_MODEL, IN_WIDTH), const),
            pl.BlockSpec((1, A_WIDTH), const),
            pl.BlockSpec((1, A_WIDTH), const),
            pl.BlockSpec((A_GROUPS * CHUNK, CHUNK), const),
            pl.BlockSpec((CHUNK, A_WIDTH), const),
        ],
        out_specs=[
            pl.BlockSpec((tm, A_WIDTH), lambda i: (i, 0)),
            pl.BlockSpec((tm, 3 * B_WIDTH), lambda i: (i, 0)),
            pl.BlockSpec((tm, 3 * C_WIDTH), lambda i: (i, 0)),
        ],
        out_shape=[
            jax.ShapeDtypeStruct((n, A_WIDTH), BF16),
            jax.ShapeDtypeStruct((n, 3 * B_WIDTH), BF16),
            jax.ShapeDtypeStruct((n, 3 * C_WIDTH), BF16),
        ],
        compiler_params=pltpu.CompilerParams(
            dimension_semantics=("parallel",), vmem_limit_bytes=VMEM_LIMIT_BYTES),
        name="inproj_gating",
    )(x2d, w_in_b, a_ln_g, a_ln_b, ws_all, bs_tab)


def _split3(x):
    hi = x.astype(BF16).astype(F32)
    r = x - hi
    mid = r.astype(BF16).astype(F32)
    return hi, mid, r - mid


def _aug_lanes(x, lane, value_first, one):
    l6 = jnp.bitwise_and(lane, HEAD_DIM - 1)
    hi, mid, lo = _split3(x)
    v0 = 0 if value_first else 3
    out = jnp.where(l6 == v0, hi, jnp.where(l6 == v0 + 1, mid, jnp.where(l6 == v0 + 2, lo, 0.0)))
    ones = (l6 >= 3 - v0) & (l6 < 6 - v0)
    return jnp.where(ones, one, out)


def _diffattn_kernel(slopes_ref, q_ref, k_ref, v_ref, lq1_ref, lk1_ref, lq2_ref, lk2_ref, g_ref,
                     o_ref, ka0_ref, ka1_ref, va_ref, acc0_ref, acc1_ref, s_ref, mx_ref, kn_ref, db_ref,
                     *, tq, tk, seq, lam_init, head0, skip):
    h = pl.program_id(1) + head0
    qi = pl.program_id(2)
    c = slopes_ref[h] * LOG2E
    fill = 512
    lane_f = lax.broadcasted_iota(jnp.int32, (fill, LANES), 1)
    row_f = lax.broadcasted_iota(jnp.int32, (fill, LANES), 0)

    @pl.when(qi == 0)
    def _():
        def fill_body(r, carry):
            start = pl.multiple_of(r * fill, fill)
            pos = (row_f + r * fill).astype(F32)
            aug = _aug_lanes(c * pos, lane_f, False, 1.0).astype(BF16)
            kk = k_ref[pl.ds(start, fill), :]
            ka0_ref[pl.ds(start, fill), :] = jnp.where(lane_f < HEAD_DIM, kk, aug)
            ka1_ref[pl.ds(start, fill), :] = jnp.where(lane_f < HEAD_DIM, aug, kk)
            va_ref[pl.ds(start, fill), 0:LANES] = v_ref[pl.ds(start, fill), :]
            va_ref[pl.ds(start, fill), LANES:2 * LANES] = jnp.ones((fill, LANES), BF16)
            if not skip:
                return carry
            sq = jnp.square(kk.astype(F32))
            n0 = jnp.sum(jnp.where(lane_f < HEAD_DIM, sq, 0.0), axis=-1, keepdims=True)
            n1 = jnp.sum(jnp.where(lane_f < HEAD_DIM, 0.0, sq), axis=-1, keepdims=True)
            return jnp.maximum(carry[0], n0), jnp.maximum(carry[1], n1)
        init = (jnp.zeros((fill, 1), F32),) * 2 if skip else 0
        norms = lax.fori_loop(0, seq // fill, fill_body, init)
        rel = (lax.broadcasted_iota(jnp.int32, (tq, tk), 0)
               - lax.broadcasted_iota(jnp.int32, (tq, tk), 1))
        db_ref[...] = c * jnp.abs(rel).astype(F32)
        if skip:
            for c2 in range(2):
                kn_ref[c2] = jnp.broadcast_to(jnp.max(norms[c2], axis=0, keepdims=True), (8, LANES))

    q = q_ref[...]
    lane = lax.broadcasted_iota(jnp.int32, (tq, LANES), 1)
    low = lane < HEAD_DIM
    t = (lax.broadcasted_iota(jnp.int32, (tq, LANES), 0) + qi * tq).astype(F32)
    aug_before = _aug_lanes(-(c * t), lane, True, 1.0)
    zero = jnp.zeros_like(q)

    def variants(aug):
        return jnp.where(low, q, aug), jnp.where(low, aug, q)

    q_before = variants(aug_before.astype(BF16))
    q_after = variants((-aug_before).astype(BF16))
    q_diag = variants(zero)

    acc0_ref[...] = jnp.zeros_like(acc0_ref)
    acc1_ref[...] = jnp.zeros_like(acc1_ref)

    nkv = seq // tk
    jd = (qi * tq) // tk
    ksub = tk // MXU_TILE

    def kv_start(i):
        j = i + (i >= jd).astype(jnp.int32)
        return j, pl.multiple_of(j * tk, tk)

    def scores_into(buf, start, qv, bias=None):
        for c2, ka_ref in enumerate((ka0_ref, ka1_ref)):
            mx = None
            for u in range(ksub):
                keys = slice(u * MXU_TILE, (u + 1) * MXU_TILE)
                s = _dot_nt(qv[c2], ka_ref[pl.ds(start + u * MXU_TILE, MXU_TILE), :])
                if bias is not None:
                    s = s - bias[:, keys]
                s_ref[buf, c2, :, keys] = s
                for n in range(MXU_TILE // LANES):
                    tile = s[:, n * LANES:(n + 1) * LANES]
                    mx = tile if mx is None else jnp.maximum(mx, tile)
            mx_ref[buf, c2] = mx

    def scores_offdiag(i, buf):
        j, start = kv_start(i)
        before = j < jd
        qv = tuple(jnp.where(before, qb, qa) for qb, qa in zip(q_before, q_after))
        scores_into(buf, start, qv)

    def step(buf, c2, m, acc_ref, start):
        m_new = jnp.maximum(m, jnp.max(mx_ref[buf, c2], axis=-1, keepdims=True))
        pv = None
        for u in range(ksub):
            keys = slice(u * MXU_TILE, (u + 1) * MXU_TILE)
            p = jnp.exp2((s_ref[buf, c2, :, keys] - m_new).astype(BF16))
            part = _dot(p, va_ref[pl.ds(start + u * MXU_TILE, MXU_TILE), :])
            pv = part if pv is None else pv + part
        acc_ref[...] = jnp.exp2(m - m_new) * acc_ref[...] + pv
        return m_new

    def consume(buf, start, carry):
        return (step(buf, 0, carry[0], acc0_ref, start), step(buf, 1, carry[1], acc1_ref, start))

    neg = jnp.full((tq, 1), -jnp.inf, F32)
    diag_start = pl.multiple_of(jd * tk, tk)
    diag_bias = db_ref
    carry = (neg, neg)
    if not skip:
        scores_offdiag(jnp.int32(0), 0)
        for i in range(nkv):
            nxt = (i + 1) % SCORE_BUFFERS
            if i + 1 < nkv - 1:
                scores_offdiag(jnp.int32(i + 1), nxt)
            elif i + 1 == nkv - 1:
                scores_into(nxt, diag_start, q_diag, diag_bias)
            start = diag_start if i == nkv - 1 else kv_start(jnp.int32(i))[1]
            carry = consume(i % SCORE_BUFFERS, start, carry)
    else:
        def scores_block(j, buf):
            before = j < jd
            qv = tuple(jnp.where(before, qb, qa) for qb, qa in zip(q_before, q_after))
            scores_into(buf, pl.multiple_of(j * tk, tk), qv)

        first = jd == 0
        last = jd == nkv - 1
        n1 = jnp.where(first, 1, jd - 1)
        n2 = jnp.where(first, 2, jnp.where(last, nkv - 3, jd + 1))
        scores_into(0, diag_start, q_diag, diag_bias)
        scores_block(n1, 1)
        carry = consume(0, diag_start, carry)
        scores_block(n2, 0)
        carry = consume(1, pl.multiple_of(n1 * tk, tk), carry)
        carry = consume(0, pl.multiple_of(n2 * tk, tk), carry)

        sq = jnp.square(q.astype(F32))
        q2 = (jnp.max(jnp.sum(jnp.where(low, sq, 0.0), axis=-1, keepdims=True), axis=0, keepdims=True),
              jnp.max(jnp.sum(jnp.where(low, 0.0, sq), axis=-1, keepdims=True), axis=0, keepdims=True))
        ub = jnp.maximum(jnp.sqrt(q2[0] * kn_ref[0][0:1, 0:1]), jnp.sqrt(q2[1] * kn_ref[1][0:1, 0:1]))
        ub = ub * 1.001 + 0.5
        m_min = jnp.min(jnp.minimum(carry[0], carry[1]), axis=0, keepdims=True)
        d = lax.broadcasted_iota(jnp.int32, (1, LANES), 1) + 1
        reach = ub - c * ((d - 1) * tk + 1).astype(F32) > m_min - SKIP_MARGIN
        needed = jnp.where(reach & (d <= nkv - 1), 1, 0)
        dist = jnp.sum(needed, axis=-1, keepdims=True)[0, 0]
        seen_b = jnp.where(first, 0, 1) + jnp.where(last, 1, 0)
        seen_a = jnp.where(last, 0, 1) + jnp.where(first, 1, 0)
        rem_b = jnp.maximum(jnp.minimum(dist, jd) - seen_b, 0)
        rem_a = jnp.maximum(jnp.minimum(dist, nkv - 1 - jd) - seen_a, 0)

        def extra(i, carry):
            j = jnp.where(i < rem_b, jd - seen_b - 1 - i, jd + seen_a + 1 + (i - rem_b))
            j = jnp.clip(j, 0, nkv - 1)
            scores_block(j, 0)
            return consume(0, pl.multiple_of(j * tk, tk), carry)

        lax.fori_loop(0, rem_b + rem_a, extra, carry)

    lam = (jnp.exp(jnp.sum(lq1_ref[...] * lk1_ref[...], axis=-1, keepdims=True))
           - jnp.exp(jnp.sum(lq2_ref[...] * lk2_ref[...], axis=-1, keepdims=True)) + lam_init)
    o = (acc0_ref[:, 0:LANES] / acc0_ref[:, LANES:2 * LANES]
         - lam * (acc1_ref[:, 0:LANES] / acc1_ref[:, LANES:2 * LANES]))
    ms = jnp.mean(o * o, axis=-1, keepdims=True)
    o_ref[...] = (o * lax.rsqrt(ms + LN_EPS) * g_ref[...] * (1.0 - lam_init)).astype(o_ref.dtype)


def _diffattn(hb, slopes, lq1, lk1, lq2, lk2, subln_g, *, bsz, seq, lam_init, tq, tk, head0, nheads, skip):
    n = bsz * seq
    nq = seq // tq
    const = lambda b, h, i, sl: (0, 0)
    grid_spec = pltpu.PrefetchScalarGridSpec(
        num_scalar_prefetch=1,
        grid=(bsz, nheads, nq),
        in_specs=[
            pl.BlockSpec((tq, LANES), lambda b, h, i, sl: (b * nq + i, head0 + h)),
            pl.BlockSpec((seq, LANES), lambda b, h, i, sl: (b, B_HEADS + head0 + h)),
            pl.BlockSpec((seq, LANES), lambda b, h, i, sl: (b, 2 * B_HEADS + head0 + h)),
            pl.BlockSpec((1, HEAD_DIM), const),
            pl.BlockSpec((1, HEAD_DIM), const),
            pl.BlockSpec((1, HEAD_DIM), const),
            pl.BlockSpec((1, HEAD_DIM), const),
            pl.BlockSpec((1, 2 * HEAD_DIM), const),
        ],
        out_specs=pl.BlockSpec((tq, LANES), lambda b, h, i, sl: (b * nq + i, h)),
        scratch_shapes=[pltpu.VMEM((seq, LANES), BF16), pltpu.VMEM((seq, LANES), BF16),
                        pltpu.VMEM((seq, 2 * LANES), BF16),
                        pltpu.VMEM((tq, 2 * LANES), F32), pltpu.VMEM((tq, 2 * LANES), F32),
                        pltpu.VMEM((SCORE_BUFFERS, 2, tq, tk), F32),
                        pltpu.VMEM((SCORE_BUFFERS, 2, tq, LANES), F32),
                        pltpu.VMEM((2, 8, LANES), F32),
                        pltpu.VMEM((tq, tk), F32)],
    )
    assert seq % tk == 0 and tq == tk and tk % MXU_TILE == 0 and seq // tk >= (3 if skip else 2)
    return pl.pallas_call(
        functools.partial(_diffattn_kernel, tq=tq, tk=tk, seq=seq, lam_init=lam_init, head0=head0, skip=skip),
        grid_spec=grid_spec,
        out_shape=jax.ShapeDtypeStruct((n, nheads * LANES), BF16),
        compiler_params=pltpu.CompilerParams(
            dimension_semantics=("arbitrary", "arbitrary", "arbitrary"),
            vmem_limit_bytes=VMEM_LIMIT_BYTES),
        name="diff_attention_near" if skip else "diff_attention",
    )(slopes, hb, hb, hb, lq1, lk1, lq2, lk2, subln_g)


def _na_tables(rows):
    wr = min(NA_ROWS_MAX, rows)
    nblk = rows // NA_Q_ROWS
    col = np.arange(GRID_W)
    col_start = np.clip(col - NA_COLS // 2, 0, GRID_W - NA_COLS)
    kc = np.arange(GRID_W)
    col_valid = (kc[None, :] >= col_start[:, None]) & (kc[None, :] < col_start[:, None] + NA_COLS)

    assert rows >= NA_WIN_ROWS and rows % NA_Q_ROWS == 0
    wstart, types, type_of = [], [], []
    for b in range(nblk):
        r = NA_Q_ROWS * b + np.arange(NA_Q_ROWS)
        r_start = np.clip(r - wr // 2, 0, rows - wr)
        w = int(min(r_start[0], rows - NA_WIN_ROWS))
        kr = w + np.arange(NA_WIN_ROWS)
        row_valid = (kr[None, :] >= r_start[:, None]) & (kr[None, :] < r_start[:, None] + wr)
        assert (row_valid.sum(axis=1) == wr).all()
        row_off0 = w - r + (NA_ROWS_MAX - 1)
        key = (row_off0.tobytes(), row_valid.tobytes())
        for t, (k0, _, _) in enumerate(types):
            if k0 == key:
                break
        else:
            t = len(types)
            types.append((key, row_off0, row_valid))
        wstart.append(w * GRID_W)
        type_of.append(t)
    return (np.asarray(wstart, np.int32), np.asarray(type_of, np.int32),
            [(t[1], t[2]) for t in types], col_valid)


def _na_bias_table(rpb, types, col_valid):
    nrow = rpb.shape[1]
    rp = jnp.pad(rpb.astype(F32) * LOG2E, ((0, 0), (0, 0), (GRID_W, GRID_W)))
    per_col = jnp.stack([rp[:, :, GRID_W + NA_COLS - 1 - c: 2 * GRID_W + NA_COLS - 1 - c]
                         for c in range(GRID_W)], axis=2)
    per_col = jnp.where(col_valid[None, None], per_col, NEG_BIG)
    per_col = jnp.pad(per_col, ((0, 0), (NA_WIN_ROWS, NA_WIN_ROWS), (0, 0), (0, 0)), constant_values=NEG_BIG)
    tabs = []
    for row_off0, row_valid in types:
        blocks = []
        for j in range(NA_Q_ROWS):
            lo = int(row_off0[j]) + NA_WIN_ROWS
            assert 0 <= lo and lo + NA_WIN_ROWS <= nrow + 2 * NA_WIN_ROWS
            blk = per_col[:, lo:lo + NA_WIN_ROWS]
            blk = jnp.where(row_valid[j][None, :, None, None], blk, NEG_BIG)
            blocks.append(jnp.transpose(blk, (0, 2, 1, 3)).reshape(C_HEADS, GRID_W, NA_WIN_ROWS * GRID_W))
        tabs.append(jnp.concatenate(blocks, axis=1))
    return jnp.stack(tabs)


def _natten_kernel(wstart_ref, type_ref, q_ref, k_ref, v_ref, tab_ref, o_ref, *, tq, win):
    del type_ref
    i = pl.program_id(1)
    start = pl.multiple_of(wstart_ref[i], GRID_W)
    lane = lax.broadcasted_iota(jnp.int32, (tq, LANES), 1)
    low = lane < HEAD_DIM
    for pair in range(C_HEADS // 2):
        cols = slice(pair * LANES, (pair + 1) * LANES)
        qp = q_ref[:, cols]
        kp = k_ref[pl.ds(start, win), cols]
        vp = jnp.concatenate([v_ref[pl.ds(start, win), cols], jnp.ones((win, LANES), BF16)], axis=1)
        zero = jnp.zeros_like(qp)
        outs = []
        for hh in range(2):
            qz = jnp.where(low if hh == 0 else jnp.logical_not(low), qp, zero)
            s = _dot_nt(qz, kp) + tab_ref[0, 2 * pair + hh]
            m = jnp.max(s, axis=-1, keepdims=True)
            p = jnp.exp2((s - m).astype(BF16))
            pv = _dot(p, vp)
            outs.append(pv[:, 0:LANES] / pv[:, LANES:2 * LANES])
        o_ref[:, cols] = jnp.where(low, outs[0], outs[1]).astype(o_ref.dtype)


def _natten(hc, tab, wstart, type_of, *, bsz, seq):
    n = bsz * seq
    tq = NA_Q_ROWS * GRID_W
    win = NA_WIN_ROWS * GRID_W
    nblk = seq // tq
    grid_spec = pltpu.PrefetchScalarGridSpec(
        num_scalar_prefetch=2,
        grid=(bsz, nblk),
        in_specs=[
            pl.BlockSpec((tq, C_WIDTH), lambda b, i, ws, ty: (b * nblk + i, 0)),
            pl.BlockSpec((seq, C_WIDTH), lambda b, i, ws, ty: (b, 1)),
            pl.BlockSpec((seq, C_WIDTH), lambda b, i, ws, ty: (b, 2)),
            pl.BlockSpec((1, C_HEADS, tq, win), lambda b, i, ws, ty: (ty[i], 0, 0, 0)),
        ],
        out_specs=pl.BlockSpec((tq, C_WIDTH), lambda b, i, ws, ty: (b * nblk + i, 0)),
    )
    return pl.pallas_call(
        functools.partial(_natten_kernel, tq=tq, win=win),
        grid_spec=grid_spec,
        out_shape=jax.ShapeDtypeStruct((n, C_WIDTH), BF16),
        compiler_params=pltpu.CompilerParams(
            dimension_semantics=("parallel", "parallel"), vmem_limit_bytes=VMEM_LIMIT_BYTES),
        name="neighbourhood_attention",
    )(wstart, type_of, hc, hc, hc, tab)


def _outffn_kernel(x_ref, oa_ref, ob0_ref, ob1_ref, oc_ref, wo_ref, wg_ref, wu_ref, wd_ref, lng_ref, lnb_ref,
                   y_ref):
    tm = x_ref.shape[0]
    halves = (slice(0, tm // 2), slice(tm // 2, tm))
    x1, x1b = [], []
    for rows in halves:
        mix = (_dot(oa_ref[rows, :], wo_ref[0:A_WIDTH, :])
               + _dot(jnp.concatenate([ob0_ref[rows, :], ob1_ref[rows, :]], axis=1),
                      wo_ref[A_WIDTH:A_WIDTH + B_WIDTH, :])
               + _dot(oc_ref[rows, :], wo_ref[A_WIDTH + B_WIDTH:MIX_WIDTH, :]))
        x1.append(_layer_norm(DEEPNORM_ALPHA * x_ref[rows, :] + mix, lng_ref[0:1, :], lnb_ref[0:1, :]))
        x1b.append(x1[-1].astype(BF16))
    ff = [None, None]
    off = 0
    for width in FF_CHUNKS:
        cols = slice(off, off + width)
        for i in range(2):
            gate = _dot(x1b[i], wg_ref[:, cols])
            up = _dot(x1b[i], wu_ref[:, cols])
            part = _dot((jax.nn.silu(gate) * up).astype(BF16), wd_ref[cols, :])
            ff[i] = part if ff[i] is None else ff[i] + part
        off += width
    for i, rows in enumerate(halves):
        y_ref[rows, :] = _layer_norm(DEEPNORM_ALPHA * x1[i] + ff[i], lng_ref[1:2, :], lnb_ref[1:2, :])


def _outffn(x2d, oa, ob, oc, wo, wg, wu, wd, ln_g, ln_b, *, tm):
    n = x2d.shape[0]
    const = lambda i: (0, 0)
    row = lambda i: (i, 0)
    resident = functools.partial(pl.BlockSpec, index_map=const, pipeline_mode=pl.Buffered(1))
    return pl.pallas_call(
        _outffn_kernel,
        grid=(n // tm,),
        in_specs=[
            pl.BlockSpec((tm, D_MODEL), row),
            pl.BlockSpec((tm, A_WIDTH), row),
            pl.BlockSpec((tm, SKIP_HEADS * LANES), row),
            pl.BlockSpec((tm, B_WIDTH - SKIP_HEADS * LANES), row),
            pl.BlockSpec((tm, C_WIDTH), row),
            resident((MIX_WIDTH, D_MODEL)),
            resident((D_MODEL, D_FF)),
            resident((D_MODEL, D_FF)),
            resident((D_FF, D_MODEL)),
            pl.BlockSpec((2, D_MODEL), const),
            pl.BlockSpec((2, D_MODEL), const),
        ],
        out_specs=pl.BlockSpec((tm, D_MODEL), row),
        out_shape=jax.ShapeDtypeStruct((n, D_MODEL), F32),
        compiler_params=pltpu.CompilerParams(
            dimension_semantics=("parallel",), vmem_limit_bytes=VMEM_LIMIT_BYTES),
        name="outproj_ffn",
    )(x2d, oa, ob[0], ob[1], oc, wo, wg, wu, wd, ln_g, ln_b)


def _prepare_layer(l, w_in, w_out, a_ln_g, a_ln_b, a_w_s, a_b_s, c_rpb, w_gate, w_up, w_down):
    col_scale = np.ones((IN_WIDTH,), np.float32)
    col_scale[OFF_B:OFF_B + B_WIDTH] = QK_SCALE * LOG2E
    col_scale[OFF_C:OFF_C + C_WIDTH] = QK_SCALE * LOG2E
    return dict(
        w_in=(w_in[l] * col_scale).astype(BF16),
        w_out=w_out[l].astype(BF16),
        a_ln_g=a_ln_g[l][None, :],
        a_ln_b=a_ln_b[l][None, :],
        ws_all=a_w_s[l].reshape(A_GROUPS * CHUNK, CHUNK).astype(BF16),
        bs_tab=jnp.repeat(jnp.transpose(a_b_s[l]), HEAD_DIM, axis=1),
        rpb=c_rpb[l],
        w_gate=w_gate[l].astype(BF16),
        w_up=w_up[l].astype(BF16),
        w_down=w_down[l].astype(BF16),
    )


def _trunk(x, layers, lam_params, subln_g, ln_g, ln_b, slopes):
    bsz, seq, _ = x.shape
    rows = seq // GRID_W
    wstart, type_of, types, col_valid = _na_tables(rows)
    x2d = x.reshape(bsz * seq, D_MODEL)
    for l, p in enumerate(layers):
        lam_init = 0.8 - 0.6 * math.exp(-0.3 * l)
        oa, hb, hc = _inproj(x2d, p["w_in"], p["a_ln_g"], p["a_ln_b"], p["ws_all"], p["bs_tab"], tm=512)
        lq1, lk1, lq2, lk2 = (a[l][None, :] for a in lam_params)
        attn = functools.partial(_diffattn, hb, slopes, lq1, lk1, lq2, lk2, subln_g[l][None, :],
                                 bsz=bsz, seq=seq, lam_init=lam_init, tq=512, tk=512)
        ob = (attn(head0=0, nheads=SKIP_HEADS, skip=True),
              attn(head0=SKIP_HEADS, nheads=B_HEADS - SKIP_HEADS, skip=False))
        tab = _na_bias_table(p["rpb"], types, col_valid)
        oc = _natten(hc, tab, jnp.asarray(wstart), jnp.asarray(type_of), bsz=bsz, seq=seq)
        x2d = _outffn(x2d, oa, ob, oc, p["w_out"], p["w_gate"], p["w_up"], p["w_down"],
                      ln_g[l], ln_b[l], tm=1024)
    return x2d.reshape(bsz, seq, D_MODEL)


def kernel(x_prompt, x_sample, w_in, w_out, a_ln_g, a_ln_b, a_w_s, a_b_s, b_lambda_q1, b_lambda_k1,
           b_lambda_q2, b_lambda_k2, b_subln_g, c_rpb, w_gate, w_up, w_down, ln_g, ln_b):
    layers = [_prepare_layer(l, w_in, w_out, a_ln_g, a_ln_b, a_w_s, a_b_s, c_rpb, w_gate, w_up, w_down)
              for l in range(DEPTH)]
    lam_params = (b_lambda_q1, b_lambda_k1, b_lambda_q2, b_lambda_k2)
    slopes = jnp.exp2(-8.0 * jnp.arange(1, B_HEADS + 1, dtype=F32) / B_HEADS)
    y_prompt = _trunk(x_prompt, layers, lam_params, b_subln_g, ln_g, ln_b, slopes)
    y_sample = _trunk(x_sample, layers, lam_params, b_subln_g, ln_g, ln_b, slopes)
    return (y_prompt, y_sample)
```
